```python
import jax, jax.numpy as jnp
from jax import lax
import numpy as np

D_MODEL = 2048
BATCH = 2
SEQ = 4096
DEPTH = 2
DEC_BATCH = 32
DEC_SEQ = 64
PAST_LEN = 2048

CHUNK = 64
N_MIXERS = 2
N_CONV_LAYERS = (DEPTH + 1) // 2
N_ATTN_LAYERS = DEPTH // 2
CONV_WIDTH = 3
N_HEADS = 16
N_KV_HEADS = 4
HEAD_DIM = 128
N_IDX_HEADS = 16
IDX_DIM = 128
INDEX_TOPK = 256
QUERY_BLOCK = 128
D_FF = 5632
ROPE_THETA = 10000.0
LN_EPS = 1e-5
DEEPNORM_ALPHA = (2 * DEPTH) ** 0.25
DEEPNORM_BETA = (8 * DEPTH) ** -0.25
ATTN_SCALE = HEAD_DIM ** -0.5
IDX_W_SCALE = (N_IDX_HEADS ** -0.5) * (IDX_DIM ** -0.5)
Q_WIDTH = N_HEADS * HEAD_DIM
KV_WIDTH = N_KV_HEADS * HEAD_DIM
IQ_WIDTH = N_IDX_HEADS * IDX_DIM
ATTN_SPLITS = (Q_WIDTH, Q_WIDTH + KV_WIDTH, Q_WIDTH + 2 * KV_WIDTH, Q_WIDTH + 2 * KV_WIDTH + IQ_WIDTH, Q_WIDTH + 2 * KV_WIDTH + IQ_WIDTH + IDX_DIM)
ATTN_IN_WIDTH = ATTN_SPLITS[-1] + N_IDX_HEADS

kernel_name = 'hybrid_shortconv_dsa_stream_step'


def _layer_norm(x, g, b):
    xf = x.astype(jnp.float32)
    mu = jnp.mean(xf, axis=-1, keepdims=True)
    xc = xf - mu
    var = jnp.mean(xc * xc, axis=-1, keepdims=True)
    return (xc * lax.rsqrt(var + LN_EPS) * g.astype(jnp.float32) + b.astype(jnp.float32)).astype(x.dtype)


def _rope(x, pos):
    half = x.shape[-1] // 2
    inv_freq = ROPE_THETA ** (-jnp.arange(half, dtype=jnp.float32) / half)
    ang = pos.astype(jnp.float32)[:, None] * inv_freq[None, :]
    cos = jnp.cos(ang)[:, None, :].astype(x.dtype)
    sin = jnp.sin(ang)[:, None, :].astype(x.dtype)
    x1, x2 = x[..., :half], x[..., half:]
    return jnp.concatenate([x1 * cos - x2 * sin, x1 * sin + x2 * cos], axis=-1)


def _causal_dwconv3(xp, w):
    t = xp.shape[1] - (CONV_WIDTH - 1)
    return w[0] * xp[:, 0:t] + w[1] * xp[:, 1:t + 1] + w[2] * xp[:, 2:t + 2]


def _short_conv_mixer(x, prev, w_in, conv_w, w_out):
    gate_b, gate_c, val = jnp.split(x @ w_in, 3, axis=-1)
    u = jnp.concatenate([prev, gate_c * val], axis=1)
    y = (gate_b * _causal_dwconv3(u, conv_w)) @ w_out
    return y, u[:, -(CONV_WIDTH - 1):]


def _conv_glu(x, prev, w_in, conv_w, conv_b, w_down):
    g, up = jnp.split(x @ w_in, 2, axis=-1)
    gp = jnp.concatenate([prev, g], axis=1)
    h = jax.nn.silu(_causal_dwconv3(gp, conv_w) + conv_b) * up
    return h @ w_down, gp[:, -(CONV_WIDTH - 1):]


def _attn_project(x, pos, w_in, kn_g, kn_b):
    b, t = x.shape[:2]
    q, k, v, iq, ik, iw = jnp.split(x @ w_in, list(ATTN_SPLITS), axis=-1)
    q = _rope(q.reshape(b, t, N_HEADS, HEAD_DIM), pos)
    k = _rope(k.reshape(b, t, N_KV_HEADS, HEAD_DIM), pos)
    v = v.reshape(b, t, N_KV_HEADS, HEAD_DIM)
    iq = _rope(iq.reshape(b, t, N_IDX_HEADS, IDX_DIM), pos)
    ik = _rope(_layer_norm(ik, kn_g, kn_b)[:, :, None, :], pos)[:, :, 0]
    return q, k, v, iq, ik, iw * IDX_W_SCALE


def _dsa_attend(q, iq, iw, k_all, v_all, ik_all, q_pos, k_pos, topk):
    tq = q.shape[0]
    qk_idx = jnp.einsum('qhd,sd->qhs', iq, ik_all).astype(jnp.float32)
    index = jnp.einsum('qhs,qh->qs', jax.nn.relu(qk_idx), iw.astype(jnp.float32))
    admissible = (k_pos[None, :] // CHUNK) <= (q_pos[:, None] // CHUNK)
    index = jnp.where(admissible, index, -jnp.inf)
    _, sel = lax.top_k(index, topk)
    valid = jnp.take_along_axis(admissible, sel, axis=1)
    k_sel = k_all[sel]
    v_sel = v_all[sel]
    qg = q.reshape(tq, N_KV_HEADS, N_HEADS // N_KV_HEADS, HEAD_DIM)
    s = jnp.einsum('qgrd,qkgd->qgrk', qg, k_sel).astype(jnp.float32) * ATTN_SCALE
    s = jnp.where(valid[:, None, None, :], s, -jnp.inf)
    p = jax.nn.softmax(s, axis=-1).astype(v_all.dtype)
    o = jnp.einsum('qgrk,qkgd->qgrd', p, v_sel)
    return o.reshape(tq, N_HEADS * HEAD_DIM)


def _dsa_prompt(q, iq, iw, k, v, ik, topk):
    t = q.shape[1]
    nb = t // QUERY_BLOCK
    pos = jnp.arange(t, dtype=jnp.int32)

    def per_seq(q1, iq1, iw1, k1, v1, ik1):
        def blk(a):
            qb, iqb, iwb, pb = a
            return _dsa_attend(qb, iqb, iwb, k1, v1, ik1, pb, pos, topk)
        out = lax.map(blk, (q1.reshape(nb, QUERY_BLOCK, N_HEADS, HEAD_DIM),
                            iq1.reshape(nb, QUERY_BLOCK, N_IDX_HEADS, IDX_DIM),
                            iw1.reshape(nb, QUERY_BLOCK, N_IDX_HEADS),
                            pos.reshape(nb, QUERY_BLOCK)))
        return out.reshape(t, N_HEADS * HEAD_DIM)

    return jax.vmap(per_seq)(q, iq, iw, k, v, ik)


def _dsa_sample(q, iq, iw, k_all, v_all, ik_all, q_pos, topk):
    k_pos = jnp.arange(k_all.shape[1], dtype=jnp.int32)
    return lax.map(lambda a: _dsa_attend(a[0], a[1], a[2], a[3], a[4], a[5], q_pos, k_pos, topk),
                   (q, iq, iw, k_all, v_all, ik_all))


def setup_inputs(seed: int = 0) -> dict:
    key = jax.random.key(seed)
    ks = jax.random.split(key, 24)
    f32 = jnp.float32

    def nrm(k, shape, scale):
        return jax.random.normal(k, shape, f32) * scale

    d_scale = D_MODEL ** -0.5
    mix_col = jnp.concatenate([jnp.ones((2 * D_MODEL,), f32), jnp.full((D_MODEL,), DEEPNORM_BETA, f32)])
    attn_col = jnp.ones((ATTN_IN_WIDTH,), f32).at[ATTN_SPLITS[1]:ATTN_SPLITS[2]].set(DEEPNORM_BETA)
    return {
        'x_prompt': nrm(ks[0], (BATCH, SEQ, D_MODEL), 1.0),
        'x_sample': nrm(ks[1], (DEC_BATCH, DEC_SEQ, D_MODEL), 1.0),
        'state_conv_mix': nrm(ks[2], (N_CONV_LAYERS, DEC_BATCH, CONV_WIDTH - 1, D_MODEL), 0.5),
        'cache_k': nrm(ks[3], (N_ATTN_LAYERS, DEC_BATCH, PAST_LEN, N_KV_HEADS, HEAD_DIM), 1.0),
        'cache_v': nrm(ks[4], (N_ATTN_LAYERS, DEC_BATCH, PAST_LEN, N_KV_HEADS, HEAD_DIM), 0.5),
        'cache_idx_k': nrm(ks[5], (N_ATTN_LAYERS, DEC_BATCH, PAST_LEN, IDX_DIM), 1.0),
        'state_ffn_conv': nrm(ks[6], (DEPTH, DEC_BATCH, CONV_WIDTH - 1, D_FF), 0.5),
        'mix_w_in': nrm(ks[7], (N_CONV_LAYERS, D_MODEL, 3 * D_MODEL), d_scale) * mix_col,
        'mix_conv_w': nrm(ks[8], (N_CONV_LAYERS, CONV_WIDTH, D_MODEL), CONV_WIDTH ** -0.5),
        'mix_w_out': nrm(ks[9], (N_CONV_LAYERS, D_MODEL, D_MODEL), d_scale * DEEPNORM_BETA),
        'attn_w_in': nrm(ks[10], (N_ATTN_LAYERS, D_MODEL, ATTN_IN_WIDTH), d_scale) * attn_col,
        'idx_k_norm_g': 1.0 + nrm(ks[11], (N_ATTN_LAYERS, IDX_DIM), 0.01),
        'idx_k_norm_b': nrm(ks[12], (N_ATTN_LAYERS, IDX_DIM), 0.01),
        'attn_w_out': nrm(ks[13], (N_ATTN_LAYERS, Q_WIDTH, D_MODEL), (Q_WIDTH ** -0.5) * DEEPNORM_BETA),
        'ffn_w_in': nrm(ks[14], (DEPTH, D_MODEL, 2 * D_FF), d_scale * DEEPNORM_BETA),
        'ffn_conv_w': nrm(ks[15], (DEPTH, CONV_WIDTH, D_FF), CONV_WIDTH ** -0.5),
        'ffn_conv_b': nrm(ks[16], (DEPTH, D_FF), 0.01),
        'ffn_w_down': nrm(ks[17], (DEPTH, D_FF, D_MODEL), (D_FF ** -0.5) * DEEPNORM_BETA),
        'ln1_g': 1.0 + nrm(ks[18], (DEPTH, D_MODEL), 0.01),
        'ln1_b': nrm(ks[19], (DEPTH, D_MODEL), 0.01),
        'ln2_g': 1.0 + nrm(ks[20], (DEPTH, D_MODEL), 0.01),
        'ln2_b': nrm(ks[21], (DEPTH, D_MODEL), 0.01),
    }


def reference(x_prompt, x_sample, state_conv_mix, cache_k, cache_v, cache_idx_k, state_ffn_conv,
              mix_w_in, mix_conv_w, mix_w_out, attn_w_in, idx_k_norm_g, idx_k_norm_b, attn_w_out,
              ffn_w_in, ffn_conv_w, ffn_conv_b, ffn_w_down, ln1_g, ln1_b, ln2_g, ln2_b):
    bp, tp = x_prompt.shape[:2]
    ts = x_sample.shape[1]
    past = cache_k.shape[2]
    pos_p = jnp.arange(tp, dtype=jnp.int32)
    pos_s = past + jnp.arange(ts, dtype=jnp.int32)
    topk_p = min(INDEX_TOPK, tp // 4)
    topk_s = min(INDEX_TOPK, (past + ts) // 4)

    xp, xs = x_prompt, x_sample
    conv_p, conv_s = [], []
    kp, vp, ikp, ks_, vs_, iks = [], [], [], [], [], []
    ffn_p, ffn_s = [], []
    for i in range(DEPTH):
        j = i // N_MIXERS
        if i % N_MIXERS == 0:
            zero = jnp.zeros((bp, CONV_WIDTH - 1, D_MODEL), xp.dtype)
            mp, st_p = _short_conv_mixer(xp, zero, mix_w_in[j], mix_conv_w[j], mix_w_out[j])
            ms, st_s = _short_conv_mixer(xs, state_conv_mix[j], mix_w_in[j], mix_conv_w[j], mix_w_out[j])
            conv_p.append(st_p)
            conv_s.append(st_s)
        else:
            q, k, v, iq, ik, iw = _attn_project(xp, pos_p, attn_w_in[j], idx_k_norm_g[j], idx_k_norm_b[j])
            mp = _dsa_prompt(q, iq, iw, k, v, ik, topk_p) @ attn_w_out[j]
            kp.append(k)
            vp.append(v)
            ikp.append(ik)
            q, k, v, iq, ik, iw = _attn_project(xs, pos_s, attn_w_in[j], idx_k_norm_g[j], idx_k_norm_b[j])
            k_all = jnp.concatenate([cache_k[j], k], axis=1)
            v_all = jnp.concatenate([cache_v[j], v], axis=1)
            ik_all = jnp.concatenate([cache_idx_k[j], ik], axis=1)
            ms = _dsa_sample(q, iq, iw, k_all, v_all, ik_all, pos_s, topk_s) @ attn_w_out[j]
            ks_.append(k)
            vs_.append(v)
            iks.append(ik)
        xp = _layer_norm(DEEPNORM_ALPHA * xp + mp, ln1_g[i], ln1_b[i])
        xs = _layer_norm(DEEPNORM_ALPHA * xs + ms, ln1_g[i], ln1_b[i])
        zero_f = jnp.zeros((bp, CONV_WIDTH - 1, D_FF), xp.dtype)
        fp, fst_p = _conv_glu(xp, zero_f, ffn_w_in[i], ffn_conv_w[i], ffn_conv_b[i], ffn_w_down[i])
        fs, fst_s = _conv_glu(xs, state_ffn_conv[i], ffn_w_in[i], ffn_conv_w[i], ffn_conv_b[i], ffn_w_down[i])
        ffn_p.append(fst_p)
        ffn_s.append(fst_s)
        xp = _layer_norm(DEEPNORM_ALPHA * xp + fp, ln2_g[i], ln2_b[i])
        xs = _layer_norm(DEEPNORM_ALPHA * xs + fs, ln2_g[i], ln2_b[i])

    new_conv_mix_p = jnp.stack(conv_p)
    new_k_p = jnp.stack(kp)
    new_v_p = jnp.stack(vp)
    new_idx_k_p = jnp.stack(ikp)
    new_ffn_conv_p = jnp.stack(ffn_p)
    new_conv_mix_s = jnp.stack(conv_s)
    new_k_s = jnp.stack(ks_)
    new_v_s = jnp.stack(vs_)
    new_idx_k_s = jnp.stack(iks)
    new_ffn_conv_s = jnp.stack(ffn_s)
    return (xp, xs, new_conv_mix_p, new_k_p, new_v_p, new_idx_k_p, new_ffn_conv_p,
            new_conv_mix_s, new_k_s, new_v_s, new_idx_k_s, new_ffn_conv_s)
```

```python
import functools

import jax
import jax.numpy as jnp
from jax import lax
from jax.experimental import pallas as pl
from jax.experimental.pallas import tpu as pltpu

CHUNK = 64
N_HEADS = 16
N_KV_HEADS = 4
HEAD_DIM = 128
N_IDX_HEADS = 16
IDX_DIM = 128
INDEX_TOPK = 256
ROPE_THETA = 10000.0
LN_EPS = 1e-5
ATTN_SCALE = HEAD_DIM ** -0.5
IDX_W_SCALE = (N_IDX_HEADS ** -0.5) * (IDX_DIM ** -0.5)
Q_WIDTH = N_HEADS * HEAD_DIM
KV_WIDTH = N_KV_HEADS * HEAD_DIM
IQ_WIDTH = N_IDX_HEADS * IDX_DIM
GROUP = N_HEADS // N_KV_HEADS

LANES = 128
VMEM_LIMIT_BYTES = 56 * 1024 * 1024
MASK_BIAS = -1e30
MAX_INIT = -5e29
INT_MIN = -2 ** 31

F32 = jnp.float32
BF16 = jnp.bfloat16


def _dot(a, b):
    return jnp.dot(a, b, preferred_element_type=F32)


def _dot_nt(a, b):
    return lax.dot_general(a, b, (((1,), (1,)), ((), ())), preferred_element_type=F32)


def _params(*sem):
    return pltpu.CompilerParams(dimension_semantics=sem, vmem_limit_bytes=VMEM_LIMIT_BYTES)


def _row_tile(n_rows, want):
    return want if n_rows % want == 0 else n_rows


def _conv3_rows(u, prev, w_ref):
    rows = lax.broadcasted_iota(jnp.int32, u.shape, 0)
    p2, p1 = prev[0:1], prev[1:2]
    u1 = jnp.where(rows == 0, p1, pltpu.roll(u, 1, 0))
    u2 = jnp.where(rows == 0, p2, jnp.where(rows == 1, p1, pltpu.roll(u, 2, 0)))
    return w_ref[0:1, :] * u2 + w_ref[1:2, :] * u1 + w_ref[2:3, :] * u


def _segmented_conv(u, w_ref, st_ref, nst_ref, carry_ref, emit, *, seg, nseg, tiles_per_seq):
    if nseg == 1:
        i = pl.program_id(1)

        @pl.when(i % tiles_per_seq == 0)
        def _():
            carry_ref[...] = st_ref[0]

        conv = _conv3_rows(u, carry_ref[...], w_ref)
        emit(slice(0, seg), conv)
        tail = u[seg - 2:seg]
        carry_ref[...] = tail
        nst_ref[0] = tail
    else:
        for s in range(nseg):
            rs = slice(s * seg, (s + 1) * seg)
            us = u[rs]
            emit(rs, _conv3_rows(us, st_ref[s], w_ref))
            nst_ref[s] = us[seg - 2:seg]


def _mix_in_kernel(x_ref, wb_ref, wc_ref, wv_ref, cw_ref, st_ref, z_ref, nst_ref, carry_ref, **seg_kw):
    x = x_ref[...]
    u = _dot(x, wc_ref[...]) * _dot(x, wv_ref[...])
    b = _dot(x, wb_ref[...])

    def emit(rs, conv):
        z_ref[rs, :] = (b[rs] * conv).astype(z_ref.dtype)

    _segmented_conv(u, cw_ref, st_ref, nst_ref, carry_ref, emit, **seg_kw)


def _ffn_in_kernel(x_ref, wg_ref, wu_ref, cw_ref, cb_ref, st_ref, h_ref, nst_ref, carry_ref, **seg_kw):
    x = x_ref[...]
    g = _dot(x, wg_ref[...])
    up = _dot(x, wu_ref[...])
    cb = cb_ref[...]

    def emit(rs, conv):
        a = conv + cb
        h_ref[rs, :] = (a * jax.nn.sigmoid(a) * up[rs]).astype(h_ref.dtype)

    _segmented_conv(g, cw_ref, st_ref, nst_ref, carry_ref, emit, **seg_kw)


def _gated_conv_call(body, x, w, n_branches, conv_w, conv_b, state, seq_len, tm, tn):
    n, k = x.shape
    c = w.shape[1] // n_branches
    tm = _row_tile(n, tm)
    tn = tn if c % tn == 0 else c
    nct, nrt = c // tn, n // tm
    if seq_len >= tm:
        assert seq_len % tm == 0
        seg, nseg, tiles_per_seq = tm, 1, seq_len // tm
        st_spec = pl.BlockSpec((1, 2, tn), lambda j, i: (i // tiles_per_seq, 0, j))
    else:
        assert tm % seq_len == 0
        seg, nseg, tiles_per_seq = seq_len, tm // seq_len, 1
        st_spec = pl.BlockSpec((nseg, 2, tn), lambda j, i: (i, 0, j))
    w_specs = [pl.BlockSpec((k, tn), functools.partial(lambda j, i, o: (0, j + o), o=b * nct))
               for b in range(n_branches)]
    vec_specs = [pl.BlockSpec((conv_w.shape[0], tn), lambda j, i: (0, j))]
    vec_args = [conv_w]
    if conv_b is not None:
        vec_specs.append(pl.BlockSpec((1, tn), lambda j, i: (0, j)))
        vec_args.append(conv_b.reshape(1, c))
    return pl.pallas_call(
        functools.partial(body, seg=seg, nseg=nseg, tiles_per_seq=tiles_per_seq),
        grid=(nct, nrt),
        in_specs=[pl.BlockSpec((tm, k), lambda j, i: (i, 0))] + w_specs + vec_specs + [st_spec],
        out_specs=[pl.BlockSpec((tm, tn), lambda j, i: (i, j)), st_spec],
        out_shape=[jax.ShapeDtypeStruct((n, c), BF16), jax.ShapeDtypeStruct(state.shape, F32)],
        scratch_shapes=[pltpu.VMEM((2, tn), F32)],
        compiler_params=_params("arbitrary", "arbitrary"),
    )(x, *([w] * n_branches), *vec_args, state)


def _mm_ln_kernel(a_ref, w_ref, x_ref, g_ref, b_ref, o_ref, ob_ref, *acc, alpha, nk):
    def finish(y):
        r = alpha * x_ref[...] + y
        mu = jnp.mean(r, axis=-1, keepdims=True)
        rc = r - mu
        var = jnp.mean(rc * rc, axis=-1, keepdims=True)
        out = rc * lax.rsqrt(var + LN_EPS) * g_ref[...] + b_ref[...]
        o_ref[...] = out
        ob_ref[...] = out.astype(ob_ref.dtype)

    part = _dot(a_ref[...], w_ref[...])
    if nk == 1:
        finish(part)
    else:
        acc_ref, = acc
        kk = pl.program_id(1)

        @pl.when(kk == 0)
        def _():
            acc_ref[...] = part

        @pl.when(jnp.logical_and(kk > 0, kk < nk - 1))
        def _():
            acc_ref[...] += part

        @pl.when(kk == nk - 1)
        def _():
            finish(acc_ref[...] + part)


def _mm_ln_call(a, w, x, g, b, alpha, tm, tk):
    n, k = a.shape
    d = w.shape[1]
    tm = _row_tile(n, tm)
    tk = tk if k % tk == 0 else k
    nk = k // tk
    return pl.pallas_call(
        functools.partial(_mm_ln_kernel, alpha=alpha, nk=nk),
        grid=(n // tm, nk),
        in_specs=[pl.BlockSpec((tm, tk), lambda i, kk: (i, kk)),
                  pl.BlockSpec((tk, d), lambda i, kk: (kk, 0)),
                  pl.BlockSpec((tm, d), lambda i, kk: (i, 0)),
                  pl.BlockSpec((1, d), lambda i, kk: (0, 0)),
                  pl.BlockSpec((1, d), lambda i, kk: (0, 0))],
        out_specs=[pl.BlockSpec((tm, d), lambda i, kk: (i, 0)),
                   pl.BlockSpec((tm, d), lambda i, kk: (i, 0))],
        out_shape=[jax.ShapeDtypeStruct((n, d), F32), jax.ShapeDtypeStruct((n, d), BF16)],
        scratch_shapes=[pltpu.VMEM((tm, d), F32)] if nk > 1 else [],
        compiler_params=_params("arbitrary", "arbitrary"),
    )(a, w, x, g.reshape(1, d), b.reshape(1, d))


def _rope_head(y, cos, sin):
    return y * cos + pltpu.roll(y, HEAD_DIM // 2, 1) * sin


def _proj_kernel(x_ref, w_ref, cos_ref, sin_ref, *outs, rope):
    y = _dot(x_ref[...], w_ref[...])
    cos, sin = cos_ref[...], sin_ref[...]
    for h in range(y.shape[1] // HEAD_DIM):
        hs = slice(h * HEAD_DIM, (h + 1) * HEAD_DIM)
        yh = y[:, hs]
        if rope:
            yh = _rope_head(yh, cos, sin)
        for o_ref in outs:
            o_ref[:, hs] = yh.astype(o_ref.dtype)


def _proj_call(x, w, col_blocks, cos, sin, rope, out_dtypes, tm, tn):
    n, k = x.shape
    tm = _row_tile(n, tm)
    nrt = n // tm
    tab_tiles = cos.shape[0] // tm
    blocks = jnp.asarray(col_blocks, jnp.int32)
    width = len(col_blocks) * tn
    grid_spec = pltpu.PrefetchScalarGridSpec(
        num_scalar_prefetch=1,
        grid=(len(col_blocks), nrt),
        in_specs=[pl.BlockSpec((tm, k), lambda j, i, blk: (i, 0)),
                  pl.BlockSpec((k, tn), lambda j, i, blk: (0, blk[j])),
                  pl.BlockSpec((tm, HEAD_DIM), lambda j, i, blk: (i % tab_tiles, 0)),
                  pl.BlockSpec((tm, HEAD_DIM), lambda j, i, blk: (i % tab_tiles, 0))],
        out_specs=[pl.BlockSpec((tm, tn), lambda j, i, blk: (i, j)) for _ in out_dtypes],
    )

    def body(blk_ref, *refs):
        del blk_ref
        _proj_kernel(*refs, rope=rope)

    return pl.pallas_call(
        body,
        grid_spec=grid_spec,
        out_shape=[jax.ShapeDtypeStruct((n, width), dt) for dt in out_dtypes],
        compiler_params=_params("arbitrary", "arbitrary"),
    )(blocks, x, w, cos, sin)


def _idx_proj_kernel(x_ref, w_ref, cos_ref, sin_ref, g_ref, b_ref, ik_ref, ikb_ref, iw_ref):
    y = _dot(x_ref[...], w_ref[...])
    a = y[:, :IDX_DIM]
    mu = jnp.mean(a, axis=-1, keepdims=True)
    ac = a - mu
    var = jnp.mean(ac * ac, axis=-1, keepdims=True)
    a = ac * lax.rsqrt(var + LN_EPS) * g_ref[...] + b_ref[...]
    a = _rope_head(a, cos_ref[...], sin_ref[...])
    ik_ref[...] = a
    ikb_ref[...] = a.astype(ikb_ref.dtype)
    iw_ref[...] = y[:, IDX_DIM:] * IDX_W_SCALE


def _idx_proj_call(x, w, cos, sin, g, b, tm):
    n, k = x.shape
    tm = _row_tile(n, tm)
    tab_tiles = cos.shape[0] // tm
    row = lambda i: (i, 0)
    tab = lambda i: (i % tab_tiles, 0)
    fixed = lambda i: (0, 0)
    return pl.pallas_call(
        _idx_proj_kernel,
        grid=(n // tm,),
        in_specs=[pl.BlockSpec((tm, k), row), pl.BlockSpec((k, 2 * IDX_DIM), fixed),
                  pl.BlockSpec((tm, IDX_DIM), tab), pl.BlockSpec((tm, IDX_DIM), tab),
                  pl.BlockSpec((1, IDX_DIM), fixed), pl.BlockSpec((1, IDX_DIM), fixed)],
        out_specs=[pl.BlockSpec((tm, IDX_DIM), row)] * 3,
        out_shape=[jax.ShapeDtypeStruct((n, IDX_DIM), F32), jax.ShapeDtypeStruct((n, IDX_DIM), BF16),
                   jax.ShapeDtypeStruct((n, IDX_DIM), F32)],
        compiler_params=_params("arbitrary"),
    )(x, w, cos, sin, g.reshape(1, IDX_DIM), b.reshape(1, IDX_DIM))


def _dsa_kernel(q_ref, iq_ref, iw_ref, ik_ref, k_ref, v_ref, o_ref, key_ref, bias_ref,
                *, tq, tk, q_pos0, lk, topk):
    i = pl.program_id(1)
    qpos_first = q_pos0 + i * tq
    last_chunk = (qpos_first + tq - 1) // CHUNK
    kend = jnp.minimum((last_chunk + 1) * CHUNK, lk)
    nkt = (kend + tk - 1) // tk

    rows = lax.broadcasted_iota(jnp.int32, (tq, tk), 0)
    lanes = lax.broadcasted_iota(jnp.int32, (tq, tk), 1)
    qchunk = (qpos_first + rows) // CHUNK

    def admissible(kt):
        return ((kt * tk + lanes) // CHUNK) <= qchunk

    iq = iq_ref[...]
    iq_rows = jnp.concatenate([iq[:, h * IDX_DIM:(h + 1) * IDX_DIM] for h in range(N_IDX_HEADS)], axis=0)
    iw = iw_ref[...]

    def idx_body(kt, carry):
        ik_t = ik_ref[0, pl.ds(pl.multiple_of(kt * tk, tk), tk), :]
        s = _dot_nt(iq_rows, ik_t)
        acc = jnp.zeros((tq, tk), F32)
        for h in range(N_IDX_HEADS):
            acc = acc + jnp.maximum(s[h * tq:(h + 1) * tq], 0.0) * iw[:, h:h + 1]
        acc = jnp.where(admissible(kt), acc, -jnp.inf)
        bits = pltpu.bitcast(acc, jnp.int32)
        key_ref[kt] = bits ^ ((bits >> 31) & 0x7FFFFFFF)
        return carry

    lax.fori_loop(0, nkt, idx_body, 0)

    def bit_body(bi, t_u):
        cand_u = t_u | jnp.left_shift(jnp.int32(1), 31 - bi)
        cand = cand_u ^ INT_MIN

        def cnt_body(kt, c):
            ge = (key_ref[kt] >= cand).astype(jnp.int32)
            for l in range(tk // LANES):
                c = c + ge[:, l * LANES:(l + 1) * LANES]
            return c

        c = lax.fori_loop(0, nkt, cnt_body, jnp.zeros((tq, LANES), jnp.int32))
        cnt = jnp.sum(c.astype(F32), axis=1, keepdims=True)
        return jnp.where(cnt >= topk, cand_u, t_u)

    thr = lax.fori_loop(0, 32, bit_body, jnp.zeros((tq, 1), jnp.int32)) ^ INT_MIN

    def bias_body(kt, carry):
        sel = jnp.logical_and(key_ref[kt] >= thr, admissible(kt))
        bias_ref[kt] = jnp.where(sel, 0.0, MASK_BIAS)
        return carry

    lax.fori_loop(0, nkt, bias_body, 0)

    q = q_ref[...]
    for g in range(N_KV_HEADS):
        gs = slice(g * HEAD_DIM, (g + 1) * HEAD_DIM)
        q_rows = jnp.concatenate(
            [q[:, (g * GROUP + r) * HEAD_DIM:(g * GROUP + r + 1) * HEAD_DIM] for r in range(GROUP)], axis=0)

        def att_body(kt, carry):
            m, l, acc = carry
            ks = pl.ds(pl.multiple_of(kt * tk, tk), tk)
            bias = bias_ref[kt]
            s = _dot_nt(q_rows, k_ref[0, ks, gs]) * ATTN_SCALE
            s = s + jnp.concatenate([bias] * GROUP, axis=0)
            m_new = jnp.maximum(m, jnp.max(s, axis=1, keepdims=True))
            p = jnp.exp(s - m_new)
            alpha = jnp.exp(m - m_new)
            l = alpha * l + jnp.sum(p, axis=1, keepdims=True)
            acc = alpha * acc + _dot(p.astype(BF16), v_ref[0, ks, gs])
            return m_new, l, acc

        init = (jnp.full((GROUP * tq, 1), MAX_INIT, F32), jnp.zeros((GROUP * tq, 1), F32),
                jnp.zeros((GROUP * tq, HEAD_DIM), F32))
        _, l, acc = lax.fori_loop(0, nkt, att_body, init)
        og = acc / l
        for r in range(GROUP):
            hs = slice((g * GROUP + r) * HEAD_DIM, (g * GROUP + r + 1) * HEAD_DIM)
            o_ref[:, hs] = og[r * tq:(r + 1) * tq].astype(o_ref.dtype)


def _dsa_call(qiq, iw, ik_all, k_all, v_all, n_seq, tq_per_seq, q_pos0, topk, tq, tk):
    lk = ik_all.shape[1]
    assert lk % tk == 0 and tq_per_seq % tq == 0
    nqb = tq_per_seq // tq
    nkt_max = lk // tk
    qrow = lambda b, i: (b * nqb + i, 0)
    seq = lambda b, i: (b, 0, 0)
    return pl.pallas_call(
        functools.partial(_dsa_kernel, tq=tq, tk=tk, q_pos0=q_pos0, lk=lk, topk=topk),
        grid=(n_seq, nqb),
        in_specs=[pl.BlockSpec((tq, Q_WIDTH), qrow),
                  pl.BlockSpec((tq, IQ_WIDTH), lambda b, i: (b * nqb + i, Q_WIDTH // IQ_WIDTH)),
                  pl.BlockSpec((tq, IDX_DIM), qrow),
                  pl.BlockSpec((1, lk, IDX_DIM), seq),
                  pl.BlockSpec((1, lk, KV_WIDTH), seq),
                  pl.BlockSpec((1, lk, KV_WIDTH), seq)],
        out_specs=pl.BlockSpec((tq, Q_WIDTH), qrow),
        out_shape=jax.ShapeDtypeStruct((n_seq * tq_per_seq, Q_WIDTH), BF16),
        scratch_shapes=[pltpu.VMEM((nkt_max, tq, tk), jnp.int32), pltpu.VMEM((nkt_max, tq, tk), F32)],
        compiler_params=_params("arbitrary", "arbitrary"),
    )(qiq, qiq, iw, ik_all, k_all, v_all)


def _rope_tables(pos, reps):
    half = HEAD_DIM // 2
    inv_freq = ROPE_THETA ** (-jnp.arange(half, dtype=F32) / half)
    ang = pos.astype(F32)[:, None] * inv_freq[None, :]
    cos, sin = jnp.cos(ang), jnp.sin(ang)
    cos2 = jnp.concatenate([cos, cos], axis=1)
    sin2 = jnp.concatenate([-sin, sin], axis=1)
    return jnp.tile(cos2, (reps, 1)), jnp.tile(sin2, (reps, 1))


DEFAULT_CFG = dict(tm=1024, tn_mix=512, tn_ffn=512, tm_ln=512, tk_down=1408, tn_proj=512,
                   tq_p=128, tk_p=512, tk_s=256)


def _forward(x_prompt, x_sample, state_conv_mix, cache_k, cache_v, cache_idx_k, state_ffn_conv,
             mix_w_in, mix_conv_w, mix_w_out, attn_w_in, idx_k_norm_g, idx_k_norm_b, attn_w_out,
             ffn_w_in, ffn_conv_w, ffn_conv_b, ffn_w_down, ln1_g, ln1_b, ln2_g, ln2_b, cfg):
    bp, tp, d = x_prompt.shape
    bs, ts, _ = x_sample.shape
    past = cache_k.shape[2]
    depth = ln1_g.shape[0]
    d_ff = ffn_conv_b.shape[1]
    alpha = (2 * depth) ** 0.25
    topk_p = min(INDEX_TOPK, tp // 4)
    topk_s = min(INDEX_TOPK, (past + ts) // 4)
    tm = cfg["tm"]

    streams = {
        "p": [x_prompt.reshape(bp * tp, d), x_prompt.reshape(bp * tp, d).astype(BF16), tp, bp],
        "s": [x_sample.reshape(bs * ts, d), x_sample.reshape(bs * ts, d).astype(BF16), ts, bs],
    }
    out = {k: {"conv": [], "k": [], "v": [], "ik": [], "ffn": []} for k in streams}

    def tables(name):
        seq_len, n_rows = streams[name][2], streams[name][0].shape[0]
        tile = _row_tile(n_rows, tm)
        if name == "p":
            pos = jnp.arange(seq_len, dtype=jnp.int32)
        else:
            pos = past + jnp.arange(seq_len, dtype=jnp.int32)
        return _rope_tables(pos, max(1, tile // seq_len))

    for i in range(depth):
        j = i // 2
        if i % 2 == 0:
            w_in = mix_w_in[j].astype(BF16)
            w_out = mix_w_out[j].astype(BF16)
            for name, (xf, xb, seq_len, n_seq) in streams.items():
                st = jnp.zeros((n_seq, 2, d), F32) if name == "p" else state_conv_mix[j]
                z, nst = _gated_conv_call(_mix_in_kernel, xb, w_in, 3, mix_conv_w[j], None, st,
                                          seq_len, tm, cfg["tn_mix"])
                out[name]["conv"].append(nst)
                xf, xb = _mm_ln_call(z, w_out, xf, ln1_g[i], ln1_b[i], alpha, cfg["tm_ln"], d)
                streams[name][0], streams[name][1] = xf, xb
        else:
            w_in = attn_w_in[j].astype(BF16)
            w_out = attn_w_out[j].astype(BF16)
            splits = (Q_WIDTH, Q_WIDTH + KV_WIDTH, Q_WIDTH + 2 * KV_WIDTH,
                      Q_WIDTH + 2 * KV_WIDTH + IQ_WIDTH, Q_WIDTH + 2 * KV_WIDTH + IQ_WIDTH + IDX_DIM)
            w_idx = jnp.concatenate(
                [w_in[:, splits[3]:], jnp.zeros((d, IDX_DIM - N_IDX_HEADS), BF16)], axis=1)
            tn = cfg["tn_proj"]
            q_blocks = list(range(0, Q_WIDTH // tn)) + list(range(splits[2] // tn, splits[3] // tn))
            k_blocks = list(range(splits[0] // tn, splits[1] // tn))
            v_blocks = list(range(splits[1] // tn, splits[2] // tn))
            for name, (xf, xb, seq_len, n_seq) in streams.items():
                cos, sin = tables(name)
                qiq, = _proj_call(xb, w_in, q_blocks, cos, sin, True, [BF16], tm, tn)
                k_f, k_b = _proj_call(xb, w_in, k_blocks, cos, sin, True, [F32, BF16], tm, tn)
                v_f, v_b = _proj_call(xb, w_in, v_blocks, cos, sin, False, [F32, BF16], tm, tn)
                ik_f, ik_b, iw = _idx_proj_call(xb, w_idx, cos, sin, idx_k_norm_g[j], idx_k_norm_b[j], tm)
                out[name]["k"].append(k_f.reshape(n_seq, seq_len, N_KV_HEADS, HEAD_DIM))
                out[name]["v"].append(v_f.reshape(n_seq, seq_len, N_KV_HEADS, HEAD_DIM))
                out[name]["ik"].append(ik_f.reshape(n_seq, seq_len, IDX_DIM))
                if name == "p":
                    o = _dsa_call(qiq, iw, ik_b.reshape(n_seq, seq_len, IDX_DIM),
                                  k_b.reshape(n_seq, seq_len, KV_WIDTH), v_b.reshape(n_seq, seq_len, KV_WIDTH),
                                  n_seq, seq_len, 0, topk_p, min(cfg["tq_p"], seq_len), min(cfg["tk_p"], seq_len))
                else:
                    tk = cfg["tk_s"]
                    lk = -(-(past + seq_len) // tk) * tk
                    pad = lk - past - seq_len

                    def with_cache(cache, new, width):
                        parts = [cache.reshape(n_seq, past, width).astype(BF16), new.reshape(n_seq, seq_len, width)]
                        if pad:
                            parts.append(jnp.zeros((n_seq, pad, width), BF16))
                        return jnp.concatenate(parts, axis=1)

                    o = _dsa_call(qiq, iw, with_cache(cache_idx_k[j], ik_b, IDX_DIM),
                                  with_cache(cache_k[j], k_b, KV_WIDTH), with_cache(cache_v[j], v_b, KV_WIDTH),
                                  n_seq, seq_len, past, topk_s, seq_len, tk)
                xf, xb = _mm_ln_call(o, w_out, xf, ln1_g[i], ln1_b[i], alpha, cfg["tm_ln"], Q_WIDTH)
                streams[name][0], streams[name][1] = xf, xb

        w_ffn_in = ffn_w_in[i].astype(BF16)
        w_ffn_down = ffn_w_down[i].astype(BF16)
        for name, (xf, xb, seq_len, n_seq) in streams.items():
            st = jnp.zeros((n_seq, 2, d_ff), F32) if name == "p" else state_ffn_conv[i]
            h, nst = _gated_conv_call(_ffn_in_kernel, xb, w_ffn_in, 2, ffn_conv_w[i], ffn_conv_b[i], st,
                                      seq_len, tm, cfg["tn_ffn"])
            out[name]["ffn"].append(nst)
            xf, xb = _mm_ln_call(h, w_ffn_down, xf, ln2_g[i], ln2_b[i], alpha, cfg["tm_ln"], cfg["tk_down"])
            streams[name][0], streams[name][1] = xf, xb

    p, s = out["p"], out["s"]
    return (streams["p"][0].reshape(bp, tp, d), streams["s"][0].reshape(bs, ts, d),
            jnp.stack(p["conv"]), jnp.stack(p["k"]), jnp.stack(p["v"]), jnp.stack(p["ik"]), jnp.stack(p["ffn"]),
            jnp.stack(s["conv"]), jnp.stack(s["k"]), jnp.stack(s["v"]), jnp.stack(s["ik"]), jnp.stack(s["ffn"]))


def kernel(x_prompt, x_sample, state_conv_mix, cache_k, cache_v, cache_idx_k, state_ffn_conv, mix_w_in, mix_conv_w, mix_w_out, attn_w_in, idx_k_norm_g, idx_k_norm_b, attn_w_out, ffn_w_in, ffn_conv_w, ffn_conv_b, ffn_w_down, ln1_g, ln1_b, ln2_g, ln2_b):
    return _forward(x_prompt, x_sample, state_conv_mix, cache_k, cache_v, cache_idx_k, state_ffn_conv,
                    mix_w_in, mix_conv_w, mix_w_out, attn_w_in, idx_k_norm_g, idx_k_norm_b, attn_w_out,
                    ffn_w_in, ffn_conv_w, ffn_conv_b, ffn_w_down, ln1_g, ln1_b, ln2_g, ln2_b, DEFAULT_CFG)
```

```python
import functools

import jax
import jax.numpy as jnp
from jax import lax
from jax.experimental import pallas as pl
from jax.experimental.pallas import tpu as pltpu

CHUNK = 64
CHUNK_SHIFT = CHUNK.bit_length() - 1
assert 1 << CHUNK_SHIFT == CHUNK
N_HEADS = 16
N_KV_HEADS = 4
HEAD_DIM = 128
N_IDX_HEADS = 16
IDX_DIM = 128
INDEX_TOPK = 256
ROPE_THETA = 10000.0
LN_EPS = 1e-5
ATTN_SCALE = HEAD_DIM ** -0.5
IDX_W_SCALE = (N_IDX_HEADS ** -0.5) * (IDX_DIM ** -0.5)
Q_WIDTH = N_HEADS * HEAD_DIM
KV_WIDTH = N_KV_HEADS * HEAD_DIM
IQ_WIDTH = N_IDX_HEADS * IDX_DIM
GROUP = N_HEADS // N_KV_HEADS

LANES = 128
VMEM_LIMIT_BYTES = 56 * 1024 * 1024
MASK_BIAS = -1e30
MAX_INIT = -5e29
INT_MIN = -2 ** 31
LOG2_E = 1.4426950408889634

F32 = jnp.float32
BF16 = jnp.bfloat16


def _dot(a, b):
    return jnp.dot(a, b, preferred_element_type=F32)


def _dot_nt(a, b):
    return lax.dot_general(a, b, (((1,), (1,)), ((), ())), preferred_element_type=F32)


def _params(*sem):
    return pltpu.CompilerParams(dimension_semantics=sem, vmem_limit_bytes=VMEM_LIMIT_BYTES)


def _row_tile(n_rows, want):
    return want if n_rows % want == 0 else n_rows


def _conv3_rows(u, prev, w_ref):
    rows = lax.broadcasted_iota(jnp.int32, u.shape, 0)
    p2, p1 = prev[0:1], prev[1:2]
    u1 = jnp.where(rows == 0, p1, pltpu.roll(u, 1, 0))
    u2 = jnp.where(rows == 0, p2, jnp.where(rows == 1, p1, pltpu.roll(u, 2, 0)))
    return w_ref[0:1, :] * u2 + w_ref[1:2, :] * u1 + w_ref[2:3, :] * u


def _segmented_conv(u, w_ref, st_ref, nst_ref, carry_ref, emit, *, seg, nseg, tiles_per_seq):
    if nseg == 1:
        i = pl.program_id(1)

        @pl.when(i % tiles_per_seq == 0)
        def _():
            carry_ref[...] = st_ref[0]

        conv = _conv3_rows(u, carry_ref[...], w_ref)
        emit(slice(0, seg), conv)
        tail = u[seg - 2:seg]
        carry_ref[...] = tail
        nst_ref[0] = tail
    else:
        for s in range(nseg):
            rs = slice(s * seg, (s + 1) * seg)
            us = u[rs]
            emit(rs, _conv3_rows(us, st_ref[s], w_ref))
            nst_ref[s] = us[seg - 2:seg]


def _mix_in_kernel(x_ref, wb_ref, wc_ref, wv_ref, cw_ref, st_ref, z_ref, nst_ref, carry_ref, **seg_kw):
    x = x_ref[...]
    u = _dot(x, wc_ref[...]) * _dot(x, wv_ref[...])
    b = _dot(x, wb_ref[...])

    def emit(rs, conv):
        z_ref[rs, :] = (b[rs] * conv).astype(z_ref.dtype)

    _segmented_conv(u, cw_ref, st_ref, nst_ref, carry_ref, emit, **seg_kw)


def _ffn_in_kernel(x_ref, wg_ref, wu_ref, cw_ref, cb_ref, st_ref, h_ref, nst_ref, carry_ref, **seg_kw):
    x = x_ref[...]
    g = _dot(x, wg_ref[...])
    up = _dot(x, wu_ref[...])
    cb = cb_ref[...]

    def emit(rs, conv):
        a = conv + cb
        h_ref[rs, :] = (a * jax.nn.sigmoid(a) * up[rs]).astype(h_ref.dtype)

    _segmented_conv(g, cw_ref, st_ref, nst_ref, carry_ref, emit, **seg_kw)


def _gated_conv_call(body, name, x, w, n_branches, conv_w, conv_b, state, seq_len, tm, tn):
    n, k = x.shape
    c = w.shape[1] // n_branches
    tm = _row_tile(n, tm)
    tn = tn if c % tn == 0 else c
    nct, nrt = c // tn, n // tm
    if seq_len >= tm:
        assert seq_len % tm == 0
        seg, nseg, tiles_per_seq = tm, 1, seq_len // tm
        st_spec = pl.BlockSpec((1, 2, tn), lambda j, i: (i // tiles_per_seq, 0, j))
    else:
        assert tm % seq_len == 0
        seg, nseg, tiles_per_seq = seq_len, tm // seq_len, 1
        st_spec = pl.BlockSpec((nseg, 2, tn), lambda j, i: (i, 0, j))
    w_specs = [pl.BlockSpec((k, tn), functools.partial(lambda j, i, o: (0, j + o), o=b * nct))
               for b in range(n_branches)]
    vec_specs = [pl.BlockSpec((conv_w.shape[0], tn), lambda j, i: (0, j))]
    vec_args = [conv_w]
    if conv_b is not None:
        vec_specs.append(pl.BlockSpec((1, tn), lambda j, i: (0, j)))
        vec_args.append(conv_b.reshape(1, c))
    return pl.pallas_call(
        functools.partial(body, seg=seg, nseg=nseg, tiles_per_seq=tiles_per_seq),
        grid=(nct, nrt),
        in_specs=[pl.BlockSpec((tm, k), lambda j, i: (i, 0))] + w_specs + vec_specs + [st_spec],
        out_specs=[pl.BlockSpec((tm, tn), lambda j, i: (i, j)), st_spec],
        out_shape=[jax.ShapeDtypeStruct((n, c), BF16), jax.ShapeDtypeStruct(state.shape, F32)],
        scratch_shapes=[pltpu.VMEM((2, tn), F32)],
        compiler_params=_params("arbitrary", "arbitrary"),
        name=name,
    )(x, *([w] * n_branches), *vec_args, state)


def _mm_ln_kernel(a_ref, w_ref, x_ref, g_ref, b_ref, o_ref, ob_ref, *acc, alpha, nk):
    def finish(y):
        r = alpha * x_ref[...] + y
        mu = jnp.mean(r, axis=-1, keepdims=True)
        rc = r - mu
        var = jnp.mean(rc * rc, axis=-1, keepdims=True)
        out = rc * lax.rsqrt(var + LN_EPS) * g_ref[...] + b_ref[...]
        o_ref[...] = out
        ob_ref[...] = out.astype(ob_ref.dtype)

    part = _dot(a_ref[...], w_ref[...])
    if nk == 1:
        finish(part)
    else:
        acc_ref, = acc
        kk = pl.program_id(1)

        @pl.when(kk == 0)
        def _():
            acc_ref[...] = part

        @pl.when(jnp.logical_and(kk > 0, kk < nk - 1))
        def _():
            acc_ref[...] += part

        @pl.when(kk == nk - 1)
        def _():
            finish(acc_ref[...] + part)


def _mm_ln_call(a, w, x, g, b, alpha, tm, tk):
    n, k = a.shape
    d = w.shape[1]
    tm = _row_tile(n, tm)
    tk = tk if k % tk == 0 else k
    nk = k // tk
    return pl.pallas_call(
        functools.partial(_mm_ln_kernel, alpha=alpha, nk=nk),
        grid=(n // tm, nk),
        in_specs=[pl.BlockSpec((tm, tk), lambda i, kk: (i, kk)),
                  pl.BlockSpec((tk, d), lambda i, kk: (kk, 0)),
                  pl.BlockSpec((tm, d), lambda i, kk: (i, 0)),
                  pl.BlockSpec((1, d), lambda i, kk: (0, 0)),
                  pl.BlockSpec((1, d), lambda i, kk: (0, 0))],
        out_specs=[pl.BlockSpec((tm, d), lambda i, kk: (i, 0)),
                   pl.BlockSpec((tm, d), lambda i, kk: (i, 0))],
        out_shape=[jax.ShapeDtypeStruct((n, d), F32), jax.ShapeDtypeStruct((n, d), BF16)],
        scratch_shapes=[pltpu.VMEM((tm, d), F32)] if nk > 1 else [],
        compiler_params=_params("arbitrary", "arbitrary"),
        name=f"mm_ln_k{k}_rows{n}",
    )(a, w, x, g.reshape(1, d), b.reshape(1, d))


def _rope_head(y, cos, sin):
    return y * cos + pltpu.roll(y, HEAD_DIM // 2, 1) * sin


def _proj_kernel(x_ref, w_ref, cos_ref, sin_ref, *outs, rope, n_scaled):
    y = _dot(x_ref[...], w_ref[...])
    cos, sin = cos_ref[...], sin_ref[...]
    if n_scaled:
        scale = jnp.where(pl.program_id(0) < n_scaled, ATTN_SCALE * LOG2_E, 1.0)
    for h in range(y.shape[1] // HEAD_DIM):
        hs = slice(h * HEAD_DIM, (h + 1) * HEAD_DIM)
        yh = y[:, hs]
        if rope:
            yh = _rope_head(yh, cos, sin)
        if n_scaled:
            yh = yh * scale
        for o_ref in outs:
            o_ref[:, hs] = yh.astype(o_ref.dtype)


def _proj_call(x, w, col_blocks, cos, sin, rope, out_dtypes, tm, tn, n_scaled=0):
    n, k = x.shape
    tm = _row_tile(n, tm)
    nrt = n // tm
    tab_tiles = cos.shape[0] // tm
    blocks = jnp.asarray(col_blocks, jnp.int32)
    width = len(col_blocks) * tn
    grid_spec = pltpu.PrefetchScalarGridSpec(
        num_scalar_prefetch=1,
        grid=(len(col_blocks), nrt),
        in_specs=[pl.BlockSpec((tm, k), lambda j, i, blk: (i, 0)),
                  pl.BlockSpec((k, tn), lambda j, i, blk: (0, blk[j])),
                  pl.BlockSpec((tm, HEAD_DIM), lambda j, i, blk: (i % tab_tiles, 0)),
                  pl.BlockSpec((tm, HEAD_DIM), lambda j, i, blk: (i % tab_tiles, 0))],
        out_specs=[pl.BlockSpec((tm, tn), lambda j, i, blk: (i, j)) for _ in out_dtypes],
    )

    def body(blk_ref, *refs):
        del blk_ref
        _proj_kernel(*refs, rope=rope, n_scaled=n_scaled)

    return pl.pallas_call(
        body,
        grid_spec=grid_spec,
        out_shape=[jax.ShapeDtypeStruct((n, width), dt) for dt in out_dtypes],
        compiler_params=_params("arbitrary", "arbitrary"),
        name=f"proj_{len(col_blocks)}blk_rope{int(rope)}_rows{n}",
    )(blocks, x, w, cos, sin)


def _idx_proj_kernel(x_ref, w_ref, cos_ref, sin_ref, g_ref, b_ref, ik_ref, ikb_ref, iw_ref):
    y = _dot(x_ref[...], w_ref[...])
    a = y[:, :IDX_DIM]
    mu = jnp.mean(a, axis=-1, keepdims=True)
    ac = a - mu
    var = jnp.mean(ac * ac, axis=-1, keepdims=True)
    a = ac * lax.rsqrt(var + LN_EPS) * g_ref[...] + b_ref[...]
    a = _rope_head(a, cos_ref[...], sin_ref[...])
    ik_ref[...] = a
    ikb_ref[...] = a.astype(ikb_ref.dtype)
    iw_ref[...] = y[:, IDX_DIM:] * IDX_W_SCALE


def _idx_proj_call(x, w, cos, sin, g, b, tm):
    n, k = x.shape
    tm = _row_tile(n, tm)
    tab_tiles = cos.shape[0] // tm
    row = lambda i: (i, 0)
    tab = lambda i: (i % tab_tiles, 0)
    fixed = lambda i: (0, 0)
    return pl.pallas_call(
        _idx_proj_kernel,
        grid=(n // tm,),
        in_specs=[pl.BlockSpec((tm, k), row), pl.BlockSpec((k, 2 * IDX_DIM), fixed),
                  pl.BlockSpec((tm, IDX_DIM), tab), pl.BlockSpec((tm, IDX_DIM), tab),
                  pl.BlockSpec((1, IDX_DIM), fixed), pl.BlockSpec((1, IDX_DIM), fixed)],
        out_specs=[pl.BlockSpec((tm, IDX_DIM), row)] * 3,
        out_shape=[jax.ShapeDtypeStruct((n, IDX_DIM), F32), jax.ShapeDtypeStruct((n, IDX_DIM), BF16),
                   jax.ShapeDtypeStruct((n, IDX_DIM), F32)],
        compiler_params=_params("arbitrary"),
        name=f"idx_proj_rows{n}",
    )(x, w, cos, sin, g.reshape(1, IDX_DIM), b.reshape(1, IDX_DIM))


def _stack_heads(x, first, count, width):
    return jnp.concatenate([x[:, (first + r) * width:(first + r + 1) * width] for r in range(count)], axis=0)


def _admissible(kpos_first, n_keys, qpos_first, tq):
    rows = lax.broadcasted_iota(jnp.int32, (n_keys, LANES), 0)
    lanes = lax.broadcasted_iota(jnp.int32, (n_keys, LANES), 1)
    return ((kpos_first + rows) >> CHUNK_SHIFT) <= ((qpos_first + (lanes & (tq - 1))) >> CHUNK_SHIFT)


def _index_keys(ik_t, iq_rows, w_rows, adm, tq):
    s = _dot_nt(ik_t, iq_rows)
    acc = jnp.maximum(s[:, :LANES], 0.0) * w_rows[0:1]
    for j in range(1, s.shape[1] // LANES):
        acc = acc + jnp.maximum(s[:, j * LANES:(j + 1) * LANES], 0.0) * w_rows[j:j + 1]
    if tq < LANES:
        acc = acc + pltpu.roll(acc, tq, 1)
    acc = jnp.where(adm, acc, -jnp.inf)
    bits = pltpu.bitcast(acc, jnp.int32)
    return bits ^ ((bits >> 31) & 0x7FFFFFFF)


COUNT_ROWS = 64
MAX_EXACT_BF16_COUNT = 256
I16_MIN = -2 ** 15


def _fold_rows(x, tq):
    half = x.shape[0] // 2
    lanes = lax.broadcasted_iota(jnp.int32, (half, LANES), 1)
    return jnp.where(lanes < tq, x[:half], x[half:])


def _split_key(key):
    return (key >> 16).astype(jnp.int16), ((key & 0xFFFF) + I16_MIN).astype(jnp.int16)


def _count16(mask):
    ones = jnp.where(mask, jnp.bfloat16(1), jnp.bfloat16(0)).reshape(-1, COUNT_ROWS, LANES)
    c = ones[0]
    for i in range(1, ones.shape[0]):
        c = c + ones[i]
    return c


def _kth_largest_key(hi_ref, lo_ref, nkt, topk, tq):
    assert hi_ref.shape[1] // COUNT_ROWS * hi_ref.shape[0] <= MAX_EXACT_BF16_COUNT

    def total(count_tile):
        c = lax.fori_loop(0, nkt, lambda kt, c: c + count_tile(kt), jnp.zeros((COUNT_ROWS, LANES), BF16))
        cnt = jnp.sum(c.astype(F32), axis=0, keepdims=True)
        if tq < LANES:
            cnt = cnt + pltpu.roll(cnt, tq, 1)
        return cnt

    def search(ref, base):
        def bit_body(bi, t):
            cand = t | jnp.left_shift(jnp.int32(1), 15 - bi)
            c16 = (cand + I16_MIN).astype(jnp.int16)
            cnt = base + total(lambda kt: _count16(ref[kt] >= c16))
            return jnp.where(cnt >= topk, cand, t)

        return lax.fori_loop(0, 16, bit_body, jnp.zeros((1, LANES), jnp.int32))

    t_hi = search(hi_ref, 0.0) + I16_MIN
    t_hi16 = t_hi.astype(jnp.int16)
    above = total(lambda kt: _count16(hi_ref[kt] > t_hi16))

    def mask_lo(kt, carry):
        lo_ref[kt] = jnp.where(hi_ref[kt] == t_hi16, lo_ref[kt], jnp.int16(I16_MIN))
        return carry

    lax.fori_loop(0, nkt, mask_lo, 0)
    t_lo = search(lo_ref, above)
    return jnp.left_shift(t_hi, 16) | t_lo


def _select_bias(keys, thr, adm):
    return jnp.where(jnp.logical_and(keys >= thr, adm), 0.0, MASK_BIAS)


def _group_q_rows(q_ref, g):
    return jnp.concatenate([q_ref[:, (g * GROUP + r) * HEAD_DIM:(g * GROUP + r + 1) * HEAD_DIM]
                            for r in range(GROUP)], axis=0)


def _attend_groups(ms, ls, bias, k_of, v_of, q_ref, update_acc):
    tq = q_ref.shape[0]
    bias_rep = jnp.concatenate([bias] * (GROUP * tq // LANES), axis=1)

    def scores(g):
        return _dot_nt(k_of(g), _group_q_rows(q_ref, g)) + bias_rep

    new_m, new_l = [], []
    s_next = scores(0)
    for g in range(N_KV_HEADS):
        s = s_next
        if g + 1 < N_KV_HEADS:
            s_next = scores(g + 1)
        m, l, alpha, pv = _attend_tile(ms[g], ls[g], s, v_of(g))
        update_acc(g, alpha, pv)
        new_m.append(m)
        new_l.append(l)
    return tuple(new_m), tuple(new_l)


def _attend_tile(m, l, s, v_t):
    m_new = jnp.maximum(m, jnp.max(s, axis=0, keepdims=True))
    p = jnp.exp2(s - m_new)
    alpha = jnp.exp2(m - m_new)
    l_new = alpha * l + jnp.sum(p, axis=0, keepdims=True)
    pv = lax.dot_general(v_t, p.astype(BF16), (((0,), (0,)), ((), ())), preferred_element_type=F32)
    return m_new, l_new, alpha, pv


def _attend_init(tq):
    n = GROUP * tq
    return (tuple(jnp.full((1, n), MAX_INIT, F32) for _ in range(N_KV_HEADS)),
            tuple(jnp.zeros((1, n), F32) for _ in range(N_KV_HEADS)))


def _write_group(o_ref, g, l, acc, tq):
    o = (acc / l).T
    for r in range(GROUP):
        hs = slice((g * GROUP + r) * HEAD_DIM, (g * GROUP + r + 1) * HEAD_DIM)
        o_ref[:, hs] = o[r * tq:(r + 1) * tq].astype(o_ref.dtype)


def _dsa_prompt_kernel(q_ref, iq_ref, w_ref, ik_ref, k_ref, v_ref, o_ref,
                       key_ref, hi_ref, lo_ref, bias_ref, acc_ref, *, tq, tk, lk, topk):
    qpos_first = pl.program_id(1) * tq
    kend = jnp.minimum(((qpos_first + tq - 1) // CHUNK + 1) * CHUNK, lk)
    nkt = (kend + tk - 1) // tk

    def adm(kt):
        return _admissible(kt * tk, tk, qpos_first, tq)

    def key_rows(kt):
        return pl.ds(pl.multiple_of(kt * tk, tk), tk)

    iq_rows = _stack_heads(iq_ref[...], 0, N_IDX_HEADS, IDX_DIM)
    w_rows = w_ref[0]

    def idx_body(kt, carry):
        keys = _index_keys(ik_ref[0, key_rows(kt), :], iq_rows, w_rows, adm(kt), tq)
        key_ref[kt] = keys
        hi_ref[kt], lo_ref[kt] = _split_key(keys)
        return carry

    lax.fori_loop(0, nkt, idx_body, 0)
    thr = _kth_largest_key(hi_ref, lo_ref, nkt, topk, tq)

    def bias_body(kt, carry):
        bias_ref[kt] = _select_bias(key_ref[kt], thr, adm(kt))
        return carry

    lax.fori_loop(0, nkt, bias_body, 0)

    acc_ref[...] = jnp.zeros(acc_ref.shape, F32)

    def update_acc(g, alpha, pv):
        acc_ref[g] = alpha * acc_ref[g] + pv

    def att_body(kt, carry):
        ks = key_rows(kt)
        return _attend_groups(*carry, bias_ref[kt],
                              lambda g: k_ref[0, ks, g * HEAD_DIM:(g + 1) * HEAD_DIM],
                              lambda g: v_ref[0, ks, g * HEAD_DIM:(g + 1) * HEAD_DIM], q_ref, update_acc)

    _, ls = lax.fori_loop(0, nkt, att_body, _attend_init(tq))
    for g in range(N_KV_HEADS):
        _write_group(o_ref, g, ls[g], acc_ref[g], tq)


def _dsa_prompt_call(qiq, w_rows, ik, k, v, topk, tq, tk):
    n_seq, lk = ik.shape[:2]
    assert lk % tk == 0 and lk % tq == 0 and tq == LANES
    nqb = lk // tq
    qrow = lambda b, i: (b * nqb + i, 0)
    seq = lambda b, i: (b, 0, 0)
    return pl.pallas_call(
        functools.partial(_dsa_prompt_kernel, tq=tq, tk=tk, lk=lk, topk=topk),
        grid=(n_seq, nqb),
        in_specs=[pl.BlockSpec((tq, Q_WIDTH), qrow),
                  pl.BlockSpec((tq, IQ_WIDTH), lambda b, i: (b * nqb + i, Q_WIDTH // IQ_WIDTH)),
                  pl.BlockSpec((1, N_IDX_HEADS, tq), lambda b, i: (b, 0, i)),
                  pl.BlockSpec((1, lk, IDX_DIM), seq),
                  pl.BlockSpec((1, lk, KV_WIDTH), seq),
                  pl.BlockSpec((1, lk, KV_WIDTH), seq)],
        out_specs=pl.BlockSpec((tq, Q_WIDTH), qrow),
        out_shape=jax.ShapeDtypeStruct((n_seq * lk, Q_WIDTH), BF16),
        scratch_shapes=[pltpu.VMEM((lk // tk, tk, LANES), jnp.int32),
                        pltpu.VMEM((lk // tk, tk, LANES), jnp.int16), pltpu.VMEM((lk // tk, tk, LANES), jnp.int16),
                        pltpu.VMEM((lk // tk, tk, LANES), F32),
                        pltpu.VMEM((N_KV_HEADS, HEAD_DIM, GROUP * tq), F32)],
        compiler_params=_params("arbitrary", "arbitrary"),
        name="dsa_prompt",
    )(qiq, qiq, w_rows, ik, k, v)


def _dsa_sample_kernel(q_ref, iq_ref, w_ref, cik_ref, ck_ref, cv_ref, nik_ref, nk_ref, nv_ref, o_ref,
                       key_ref, keyn_ref, hi_ref, lo_ref, *, tq, tk, past, topk):
    nct = past // tk
    iq_rows = _stack_heads(iq_ref[...], 0, N_IDX_HEADS, IDX_DIM)
    w_rows = w_ref[0]

    def adm_cache(kt):
        return _admissible(kt * tk, tk, past, tq)

    adm_new = _admissible(past, tq, past, tq)

    for kt in range(nct):
        ik_t = cik_ref[0, kt * tk:(kt + 1) * tk, :].astype(BF16)
        keys = _index_keys(ik_t, iq_rows, w_rows, adm_cache(kt), tq)
        key_ref[kt] = keys
        hi_ref[kt], lo_ref[kt] = _split_key(_fold_rows(keys, tq))
    keys = _index_keys(nik_ref[...], iq_rows, w_rows, adm_new, tq)
    keyn_ref[...] = keys
    pad = jnp.full(((tk - tq) // 2, LANES), INT_MIN, jnp.int32)
    hi_ref[nct], lo_ref[nct] = _split_key(jnp.concatenate([_fold_rows(keys, tq), pad], axis=0))
    thr = _kth_largest_key(hi_ref, lo_ref, nct + 1, topk, tq)

    carry = _attend_init(tq)
    accs = [jnp.zeros((HEAD_DIM, GROUP * tq), F32) for _ in range(N_KV_HEADS)]

    def update_acc(g, alpha, pv):
        accs[g] = alpha * accs[g] + pv

    for kt in range(nct):
        def cache_rows(ref, g, kt=kt):
            return ref[0, pl.ds(kt * tk * N_KV_HEADS + g, tk, stride=N_KV_HEADS), :].astype(BF16)

        carry = _attend_groups(*carry, _select_bias(key_ref[kt], thr, adm_cache(kt)),
                               functools.partial(cache_rows, ck_ref), functools.partial(cache_rows, cv_ref),
                               q_ref, update_acc)
    carry = _attend_groups(*carry, _select_bias(keyn_ref[...], thr, adm_new),
                           lambda g: nk_ref[:, g * HEAD_DIM:(g + 1) * HEAD_DIM],
                           lambda g: nv_ref[:, g * HEAD_DIM:(g + 1) * HEAD_DIM], q_ref, update_acc)
    for g in range(N_KV_HEADS):
        _write_group(o_ref, g, carry[1][g], accs[g], tq)


def _dsa_sample_call(qiq, w_rows, cache_ik, cache_k, cache_v, new_ik, new_k, new_v, topk, tk):
    n_seq, past = cache_ik.shape[:2]
    tq = new_ik.shape[0] // n_seq
    assert past % tk == 0 and 2 * tq == LANES and (tk // 2) % COUNT_ROWS == 0
    qrow = lambda b: (b, 0)
    seq = lambda b: (b, 0, 0)
    return pl.pallas_call(
        functools.partial(_dsa_sample_kernel, tq=tq, tk=tk, past=past, topk=topk),
        grid=(n_seq,),
        in_specs=[pl.BlockSpec((tq, Q_WIDTH), qrow),
                  pl.BlockSpec((tq, IQ_WIDTH), lambda b: (b, Q_WIDTH // IQ_WIDTH)),
                  pl.BlockSpec((1,) + w_rows.shape[1:], seq),
                  pl.BlockSpec((1, past, IDX_DIM), seq),
                  pl.BlockSpec((1, past * N_KV_HEADS, HEAD_DIM), seq),
                  pl.BlockSpec((1, past * N_KV_HEADS, HEAD_DIM), seq),
                  pl.BlockSpec((tq, IDX_DIM), qrow),
                  pl.BlockSpec((tq, KV_WIDTH), qrow),
                  pl.BlockSpec((tq, KV_WIDTH), qrow)],
        out_specs=pl.BlockSpec((tq, Q_WIDTH), qrow),
        out_shape=jax.ShapeDtypeStruct((n_seq * tq, Q_WIDTH), BF16),
        scratch_shapes=[pltpu.VMEM((past // tk, tk, LANES), jnp.int32), pltpu.VMEM((tq, LANES), jnp.int32),
                        pltpu.VMEM((past // tk + 1, tk // 2, LANES), jnp.int16),
                        pltpu.VMEM((past // tk + 1, tk // 2, LANES), jnp.int16)],
        compiler_params=_params("arbitrary"),
        name="dsa_sample",
    )(qiq, qiq, w_rows, cache_ik, cache_k, cache_v, new_ik, new_k, new_v)


def _rope_tables(pos, reps):
    half = HEAD_DIM // 2
    inv_freq = ROPE_THETA ** (-jnp.arange(half, dtype=F32) / half)
    ang = pos.astype(F32)[:, None] * inv_freq[None, :]
    cos, sin = jnp.cos(ang), jnp.sin(ang)
    cos2 = jnp.concatenate([cos, cos], axis=1)
    sin2 = jnp.concatenate([-sin, sin], axis=1)
    return jnp.tile(cos2, (reps, 1)), jnp.tile(sin2, (reps, 1))


DEFAULT_CFG = dict(tm=1024, tn_mix=512, tn_ffn=512, tm_ln=512, tk_down=1408, tn_proj=512,
                   tq_p=128, tk_p=512, tk_s=1024)


def _forward(x_prompt, x_sample, state_conv_mix, cache_k, cache_v, cache_idx_k, state_ffn_conv,
             mix_w_in, mix_conv_w, mix_w_out, attn_w_in, idx_k_norm_g, idx_k_norm_b, attn_w_out,
             ffn_w_in, ffn_conv_w, ffn_conv_b, ffn_w_down, ln1_g, ln1_b, ln2_g, ln2_b, cfg):
    bp, tp, d = x_prompt.shape
    bs, ts, _ = x_sample.shape
    past = cache_k.shape[2]
    depth = ln1_g.shape[0]
    d_ff = ffn_conv_b.shape[1]
    alpha = (2 * depth) ** 0.25
    topk_p = min(INDEX_TOPK, tp // 4)
    topk_s = min(INDEX_TOPK, (past + ts) // 4)
    tm = cfg["tm"]

    streams = {
        "p": [x_prompt.reshape(bp * tp, d), x_prompt.reshape(bp * tp, d).astype(BF16), tp, bp],
        "s": [x_sample.reshape(bs * ts, d), x_sample.reshape(bs * ts, d).astype(BF16), ts, bs],
    }
    out = {k: {"conv": [], "k": [], "v": [], "ik": [], "ffn": []} for k in streams}

    def tables(name):
        seq_len, n_rows = streams[name][2], streams[name][0].shape[0]
        tile = _row_tile(n_rows, tm)
        if name == "p":
            pos = jnp.arange(seq_len, dtype=jnp.int32)
        else:
            pos = past + jnp.arange(seq_len, dtype=jnp.int32)
        return _rope_tables(pos, max(1, tile // seq_len))

    for i in range(depth):
        j = i // 2
        if i % 2 == 0:
            w_in = mix_w_in[j].astype(BF16)
            w_out = mix_w_out[j].astype(BF16)
            for name, (xf, xb, seq_len, n_seq) in streams.items():
                st = jnp.zeros((n_seq, 2, d), F32) if name == "p" else state_conv_mix[j]
                z, nst = _gated_conv_call(_mix_in_kernel, "mix_in_" + name, xb, w_in, 3, mix_conv_w[j], None, st,
                                          seq_len, tm, cfg["tn_mix"])
                out[name]["conv"].append(nst)
                xf, xb = _mm_ln_call(z, w_out, xf, ln1_g[i], ln1_b[i], alpha, cfg["tm_ln"], d)
                streams[name][0], streams[name][1] = xf, xb
        else:
            w_in = attn_w_in[j].astype(BF16)
            w_out = attn_w_out[j].astype(BF16)
            splits = (Q_WIDTH, Q_WIDTH + KV_WIDTH, Q_WIDTH + 2 * KV_WIDTH,
                      Q_WIDTH + 2 * KV_WIDTH + IQ_WIDTH, Q_WIDTH + 2 * KV_WIDTH + IQ_WIDTH + IDX_DIM)
            w_idx = jnp.concatenate(
                [w_in[:, splits[3]:], jnp.zeros((d, IDX_DIM - N_IDX_HEADS), BF16)], axis=1)
            tn = cfg["tn_proj"]
            q_blocks = list(range(0, Q_WIDTH // tn)) + list(range(splits[2] // tn, splits[3] // tn))
            k_blocks = list(range(splits[0] // tn, splits[1] // tn))
            v_blocks = list(range(splits[1] // tn, splits[2] // tn))
            for name, (xf, xb, seq_len, n_seq) in streams.items():
                cos, sin = tables(name)
                qiq, = _proj_call(xb, w_in, q_blocks, cos, sin, True, [BF16], tm, tn, n_scaled=Q_WIDTH // tn)
                k_f, k_b = _proj_call(xb, w_in, k_blocks, cos, sin, True, [F32, BF16], tm, tn)
                v_f, v_b = _proj_call(xb, w_in, v_blocks, cos, sin, False, [F32, BF16], tm, tn)
                ik_f, ik_b, iw = _idx_proj_call(xb, w_idx, cos, sin, idx_k_norm_g[j], idx_k_norm_b[j], tm)
                out[name]["k"].append(k_f.reshape(n_seq, seq_len, N_KV_HEADS, HEAD_DIM))
                out[name]["v"].append(v_f.reshape(n_seq, seq_len, N_KV_HEADS, HEAD_DIM))
                out[name]["ik"].append(ik_f.reshape(n_seq, seq_len, IDX_DIM))
                w_rows = iw[:, :N_IDX_HEADS].reshape(n_seq, seq_len, N_IDX_HEADS).transpose(0, 2, 1)
                if name == "p":
                    o = _dsa_prompt_call(qiq, w_rows, ik_b.reshape(n_seq, seq_len, IDX_DIM),
                                         k_b.reshape(n_seq, seq_len, KV_WIDTH), v_b.reshape(n_seq, seq_len, KV_WIDTH),
                                         topk_p, cfg["tq_p"], min(cfg["tk_p"], seq_len))
                else:
                    w_rows = w_rows.reshape(n_seq, N_IDX_HEADS * seq_len // LANES, LANES)
                    o = _dsa_sample_call(qiq, w_rows, cache_idx_k[j],
                                         cache_k[j].reshape(n_seq, past * N_KV_HEADS, HEAD_DIM),
                                         cache_v[j].reshape(n_seq, past * N_KV_HEADS, HEAD_DIM),
                                         ik_b, k_b, v_b, topk_s, min(cfg["tk_s"], past))
                xf, xb = _mm_ln_call(o, w_out, xf, ln1_g[i], ln1_b[i], alpha, cfg["tm_ln"], Q_WIDTH)
                streams[name][0], streams[name][1] = xf, xb

        w_ffn_in = ffn_w_in[i].astype(BF16)
        w_ffn_down = ffn_w_down[i].astype(BF16)
        for name, (xf, xb, seq_len, n_seq) in streams.items():
            st = jnp.zeros((n_seq, 2, d_ff), F32) if name == "p" else state_ffn_conv[i]
            h, nst = _gated_conv_call(_ffn_in_kernel, "ffn_in_" + name, xb, w_ffn_in, 2, ffn_conv_w[i], ffn_conv_b[i], st,
                                      seq_len, tm, cfg["tn_ffn"])
            out[name]["ffn"].append(nst)
            xf, xb = _mm_ln_call(h, w_ffn_down, xf, ln2_g[i], ln2_b[i], alpha, cfg["tm_ln"], cfg["tk_down"])
            streams[name][0], streams[name][1] = xf, xb

    p, s = out["p"], out["s"]
    return (streams["p"][0].reshape(bp, tp, d), streams["s"][0].reshape(bs, ts, d),
            jnp.stack(p["conv"]), jnp.stack(p["k"]), jnp.stack(p["v"]), jnp.stack(p["ik"]), jnp.stack(p["ffn"]),
            jnp.stack(s["conv"]), jnp.stack(s["k"]), jnp.stack(s["v"]), jnp.stack(s["ik"]), jnp.stack(s["ffn"]))


def kernel(x_prompt, x_sample, state_conv_mix, cache_k, cache_v, cache_idx_k, state_ffn_conv, mix_w_in, mix_conv_w, mix_w_out, attn_w_in, idx_k_norm_g, idx_k_norm_b, attn_w_out, ffn_w_in, ffn_conv_w, ffn_conv_b, ffn_w_down, ln1_g, ln1_b, ln2_g, ln2_b):
    return _forward(x_prompt, x_sample, state_conv_mix, cache_k, cache_v, cache_idx_k, state_ffn_conv,
                    mix_w_in, mix_conv_w, mix_w_out, attn_w_in, idx_k_norm_g, idx_k_norm_b, attn_w_out,
                    ffn_w_in, ffn_conv_w, ffn_conv_b, ffn_w_down, ln1_g, ln1_b, ln2_g, ln2_b, DEFAULT_CFG)
```

```python
import functools

import jax
import jax.numpy as jnp
from jax import lax
from jax.experimental import pallas as pl
from jax.experimental.pallas import tpu as pltpu

CHUNK = 64
CHUNK_SHIFT = CHUNK.bit_length() - 1
assert 1 << CHUNK_SHIFT == CHUNK
N_HEADS = 16
N_KV_HEADS = 4
HEAD_DIM = 128
N_IDX_HEADS = 16
IDX_DIM = 128
INDEX_TOPK = 256
ROPE_THETA = 10000.0
LN_EPS = 1e-5
ATTN_SCALE = HEAD_DIM ** -0.5
IDX_W_SCALE = (N_IDX_HEADS ** -0.5) * (IDX_DIM ** -0.5)
Q_WIDTH = N_HEADS * HEAD_DIM
KV_WIDTH = N_KV_HEADS * HEAD_DIM
IQ_WIDTH = N_IDX_HEADS * IDX_DIM
GROUP = N_HEADS // N_KV_HEADS

LANES = 128
VMEM_LIMIT_BYTES = 56 * 1024 * 1024
MASK_BIAS = -1e30
MAX_INIT = -5e29
INT_MIN = -2 ** 31
LOG2_E = 1.4426950408889634

F32 = jnp.float32
BF16 = jnp.bfloat16


def _dot(a, b):
    return jnp.dot(a, b, preferred_element_type=F32)


def _dot_nt(a, b):
    return lax.dot_general(a, b, (((1,), (1,)), ((), ())), preferred_element_type=F32)


def _params(*sem):
    return pltpu.CompilerParams(dimension_semantics=sem, vmem_limit_bytes=VMEM_LIMIT_BYTES)


def _row_tile(n_rows, want):
    return want if n_rows % want == 0 else n_rows


def _conv3_rows(u, prev, w):
    rows = lax.broadcasted_iota(jnp.int32, u.shape, 0)
    p2, p1 = prev[0:1], prev[1:2]
    u1 = jnp.where(rows == 0, p1, pltpu.roll(u, 1, 0))
    u2 = jnp.where(rows == 0, p2, jnp.where(rows == 1, p1, pltpu.roll(u, 2, 0)))
    return w[0:1] * u2 + w[1:2] * u1 + w[2:3] * u


def _segmented_conv(u, cs, w_ref, st_ref, nst_ref, carry_ref, emit, *, seg, nseg, tiles_per_seq):
    w = w_ref[:, cs]
    if nseg == 1:
        emit(slice(0, seg), _conv3_rows(u, carry_ref[:, cs], w))
        tail = u[seg - 2:seg]
        carry_ref[:, cs] = tail
        nst_ref[0, :, cs] = tail
    else:
        for s in range(nseg):
            rs = slice(s * seg, (s + 1) * seg)
            us = u[rs]
            emit(rs, _conv3_rows(us, st_ref[s, :, cs], w))
            nst_ref[s, :, cs] = us[seg - 2:seg]


def _chunked_gated_conv(matmuls, epilogue, st_ref, carry_ref, n_cols, chunk, seg_kw):
    if seg_kw["nseg"] == 1:
        @pl.when(pl.program_id(1) % seg_kw["tiles_per_seq"] == 0)
        def _():
            carry_ref[...] = st_ref[0]

    chunks = [slice(c, c + chunk) for c in range(0, n_cols, chunk)]
    nxt = matmuls(chunks[0])
    for c, cs in enumerate(chunks):
        cur = nxt
        if c + 1 < len(chunks):
            nxt = matmuls(chunks[c + 1])
        epilogue(cs, cur)


def _mix_in_kernel(x_ref, wb_ref, wc_ref, wv_ref, cw_ref, st_ref, z_ref, nst_ref, carry_ref, *, chunk, **seg_kw):
    x = x_ref[...]

    def matmuls(cs):
        return _dot(x, wc_ref[:, cs]) * _dot(x, wv_ref[:, cs]), _dot(x, wb_ref[:, cs])

    def epilogue(cs, res):
        u, b = res

        def emit(rs, conv):
            z_ref[rs, cs] = (b[rs] * conv).astype(z_ref.dtype)

        _segmented_conv(u, cs, cw_ref, st_ref, nst_ref, carry_ref, emit, **seg_kw)

    _chunked_gated_conv(matmuls, epilogue, st_ref, carry_ref, z_ref.shape[1], chunk, seg_kw)


def _ffn_in_kernel(x_ref, wg_ref, wu_ref, cw_ref, cb_ref, st_ref, h_ref, nst_ref, carry_ref, *, chunk, **seg_kw):
    x = x_ref[...]

    def matmuls(cs):
        return _dot(x, wg_ref[:, cs]), _dot(x, wu_ref[:, cs])

    def epilogue(cs, res):
        g, up = res
        cb = cb_ref[:, cs]

        def emit(rs, conv):
            a = conv + cb
            h_ref[rs, cs] = (a * jax.nn.sigmoid(a) * up[rs]).astype(h_ref.dtype)

        _segmented_conv(g, cs, cw_ref, st_ref, nst_ref, carry_ref, emit, **seg_kw)

    _chunked_gated_conv(matmuls, epilogue, st_ref, carry_ref, h_ref.shape[1], chunk, seg_kw)


def _gated_conv_call(body, name, x, w, layer, n_branches, conv_w, conv_b, state, seq_len, tm, tn, chunk):
    n, k = x.shape
    c = w.shape[2] // n_branches
    tm = _row_tile(n, tm)
    tn = tn if c % tn == 0 else c
    nct, nrt = c // tn, n // tm
    if seq_len >= tm:
        assert seq_len % tm == 0
        seg, nseg, tiles_per_seq = tm, 1, seq_len // tm
        st_spec = pl.BlockSpec((1, 2, tn), lambda j, i: (i // tiles_per_seq, 0, j))
    else:
        assert tm % seq_len == 0
        seg, nseg, tiles_per_seq = seq_len, tm // seq_len, 1
        st_spec = pl.BlockSpec((nseg, 2, tn), lambda j, i: (i, 0, j))
    chunk = chunk if (tn % chunk == 0 and nseg == 1) else tn
    w_specs = [pl.BlockSpec((None, k, tn), functools.partial(lambda j, i, o: (layer, 0, j + o), o=b * nct))
               for b in range(n_branches)]
    vec_specs = [pl.BlockSpec((conv_w.shape[0], tn), lambda j, i: (0, j))]
    vec_args = [conv_w]
    if conv_b is not None:
        vec_specs.append(pl.BlockSpec((1, tn), lambda j, i: (0, j)))
        vec_args.append(conv_b.reshape(1, c))
    return pl.pallas_call(
        functools.partial(body, chunk=chunk, seg=seg, nseg=nseg, tiles_per_seq=tiles_per_seq),
        grid=(nct, nrt),
        in_specs=[pl.BlockSpec((tm, k), lambda j, i: (i, 0))] + w_specs + vec_specs + [st_spec],
        out_specs=[pl.BlockSpec((tm, tn), lambda j, i: (i, j)), st_spec],
        out_shape=[jax.ShapeDtypeStruct((n, c), BF16), jax.ShapeDtypeStruct(state.shape, F32)],
        scratch_shapes=[pltpu.VMEM((2, tn), F32)],
        compiler_params=_params("arbitrary", "arbitrary"),
        name=name,
    )(x, *([w] * n_branches), *vec_args, state)


def _mm_ln_kernel(a_ref, w_ref, x_ref, g_ref, b_ref, o_ref, ob_ref, *acc, alpha, nk):
    def finish(y):
        r = alpha * x_ref[...] + y
        mu = jnp.mean(r, axis=-1, keepdims=True)
        rc = r - mu
        var = jnp.mean(rc * rc, axis=-1, keepdims=True)
        out = rc * lax.rsqrt(var + LN_EPS) * g_ref[...] + b_ref[...]
        o_ref[...] = out
        ob_ref[...] = out.astype(ob_ref.dtype)

    part = _dot(a_ref[...], w_ref[...])
    if nk == 1:
        finish(part)
    else:
        acc_ref, = acc
        kk = pl.program_id(1)

        @pl.when(kk == 0)
        def _():
            acc_ref[...] = part

        @pl.when(jnp.logical_and(kk > 0, kk < nk - 1))
        def _():
            acc_ref[...] += part

        @pl.when(kk == nk - 1)
        def _():
            finish(acc_ref[...] + part)


def _mm_ln_call(a, w, layer, x, g, b, alpha, tm, tk):
    n, k = a.shape
    d = w.shape[2]
    tm = _row_tile(n, tm)
    tk = tk if k % tk == 0 else k
    nk = k // tk
    return pl.pallas_call(
        functools.partial(_mm_ln_kernel, alpha=alpha, nk=nk),
        grid=(n // tm, nk),
        in_specs=[pl.BlockSpec((tm, tk), lambda i, kk: (i, kk)),
                  pl.BlockSpec((None, tk, d), lambda i, kk: (layer, kk, 0)),
                  pl.BlockSpec((tm, d), lambda i, kk: (i, 0)),
                  pl.BlockSpec((1, d), lambda i, kk: (0, 0)),
                  pl.BlockSpec((1, d), lambda i, kk: (0, 0))],
        out_specs=[pl.BlockSpec((tm, d), lambda i, kk: (i, 0)),
                   pl.BlockSpec((tm, d), lambda i, kk: (i, 0))],
        out_shape=[jax.ShapeDtypeStruct((n, d), F32), jax.ShapeDtypeStruct((n, d), BF16)],
        scratch_shapes=[pltpu.VMEM((tm, d), F32)] if nk > 1 else [],
        compiler_params=_params("arbitrary", "arbitrary"),
        name=f"mm_ln_k{k}_rows{n}",
    )(a, w, x, g.reshape(1, d), b.reshape(1, d))


def _rope_head(y, cos, sin):
    return y * cos + pltpu.roll(y, HEAD_DIM // 2, 1) * sin


def _proj_kernel(x_ref, w_ref, cos_ref, sin_ref, *outs, rope, n_scaled):
    y = _dot(x_ref[...], w_ref[...])
    cos, sin = cos_ref[...], sin_ref[...]
    if n_scaled:
        scale = jnp.where(pl.program_id(0) < n_scaled, ATTN_SCALE * LOG2_E, 1.0)
    for h in range(y.shape[1] // HEAD_DIM):
        hs = slice(h * HEAD_DIM, (h + 1) * HEAD_DIM)
        yh = y[:, hs]
        if rope:
            yh = _rope_head(yh, cos, sin)
        if n_scaled:
            yh = yh * scale
        for o_ref in outs:
            o_ref[:, hs] = yh.astype(o_ref.dtype)


def _proj_call(x, w, col_blocks, cos, sin, rope, out_dtypes, tm, tn, n_scaled=0):
    n, k = x.shape
    tm = _row_tile(n, tm)
    nrt = n // tm
    tab_tiles = cos.shape[0] // tm
    blocks = jnp.asarray(col_blocks, jnp.int32)
    width = len(col_blocks) * tn
    grid_spec = pltpu.PrefetchScalarGridSpec(
        num_scalar_prefetch=1,
        grid=(len(col_blocks), nrt),
        in_specs=[pl.BlockSpec((tm, k), lambda j, i, blk: (i, 0)),
                  pl.BlockSpec((k, tn), lambda j, i, blk: (0, blk[j])),
                  pl.BlockSpec((tm, HEAD_DIM), lambda j, i, blk: (i % tab_tiles, 0)),
                  pl.BlockSpec((tm, HEAD_DIM), lambda j, i, blk: (i % tab_tiles, 0))],
        out_specs=[pl.BlockSpec((tm, tn), lambda j, i, blk: (i, j)) for _ in out_dtypes],
    )

    def body(blk_ref, *refs):
        del blk_ref
        _proj_kernel(*refs, rope=rope, n_scaled=n_scaled)

    return pl.pallas_call(
        body,
        grid_spec=grid_spec,
        out_shape=[jax.ShapeDtypeStruct((n, width), dt) for dt in out_dtypes],
        compiler_params=_params("arbitrary", "arbitrary"),
        name=f"proj_{len(col_blocks)}blk_rope{int(rope)}_rows{n}",
    )(blocks, x, w, cos, sin)


def _idx_proj_kernel(x_ref, w_ref, cos_ref, sin_ref, g_ref, b_ref, ik_ref, ikb_ref, iw_ref):
    y = _dot(x_ref[...], w_ref[...])
    a = y[:, :IDX_DIM]
    mu = jnp.mean(a, axis=-1, keepdims=True)
    ac = a - mu
    var = jnp.mean(ac * ac, axis=-1, keepdims=True)
    a = ac * lax.rsqrt(var + LN_EPS) * g_ref[...] + b_ref[...]
    a = _rope_head(a, cos_ref[...], sin_ref[...])
    ik_ref[...] = a
    ikb_ref[...] = a.astype(ikb_ref.dtype)
    iw_ref[...] = y[:, IDX_DIM:] * IDX_W_SCALE


def _idx_proj_call(x, w, cos, sin, g, b, tm):
    n, k = x.shape
    tm = _row_tile(n, tm)
    tab_tiles = cos.shape[0] // tm
    row = lambda i: (i, 0)
    tab = lambda i: (i % tab_tiles, 0)
    fixed = lambda i: (0, 0)
    return pl.pallas_call(
        _idx_proj_kernel,
        grid=(n // tm,),
        in_specs=[pl.BlockSpec((tm, k), row), pl.BlockSpec((k, 2 * IDX_DIM), fixed),
                  pl.BlockSpec((tm, IDX_DIM), tab), pl.BlockSpec((tm, IDX_DIM), tab),
                  pl.BlockSpec((1, IDX_DIM), fixed), pl.BlockSpec((1, IDX_DIM), fixed)],
        out_specs=[pl.BlockSpec((tm, IDX_DIM), row)] * 3,
        out_shape=[jax.ShapeDtypeStruct((n, IDX_DIM), F32), jax.ShapeDtypeStruct((n, IDX_DIM), BF16),
                   jax.ShapeDtypeStruct((n, IDX_DIM), F32)],
        compiler_params=_params("arbitrary"),
        name=f"idx_proj_rows{n}",
    )(x, w, cos, sin, g.reshape(1, IDX_DIM), b.reshape(1, IDX_DIM))


def _stack_heads(x, first, count, width):
    return jnp.concatenate([x[:, (first + r) * width:(first + r + 1) * width] for r in range(count)], axis=0)


def _admissible(kpos_first, n_keys, qpos_first, tq):
    rows = lax.broadcasted_iota(jnp.int32, (n_keys, LANES), 0)
    lanes = lax.broadcasted_iota(jnp.int32, (n_keys, LANES), 1)
    return ((kpos_first + rows) >> CHUNK_SHIFT) <= ((qpos_first + (lanes & (tq - 1))) >> CHUNK_SHIFT)


def _index_keys(ik_t, iq_rows, w_rows, adm, tq):
    s = _dot_nt(ik_t, iq_rows)
    acc = jnp.maximum(s[:, :LANES], 0.0) * w_rows[0:1]
    for j in range(1, s.shape[1] // LANES):
        acc = acc + jnp.maximum(s[:, j * LANES:(j + 1) * LANES], 0.0) * w_rows[j:j + 1]
    if tq < LANES:
        acc = acc + pltpu.roll(acc, tq, 1)
    acc = jnp.where(adm, acc, -jnp.inf)
    bits = pltpu.bitcast(acc, jnp.int32)
    return bits ^ ((bits >> 31) & 0x7FFFFFFF)


COUNT_ROWS = 64
MAX_EXACT_BF16_COUNT = 256
I16_MIN = -2 ** 15


def _fold_rows(x, tq):
    half = x.shape[0] // 2
    lanes = lax.broadcasted_iota(jnp.int32, (half, LANES), 1)
    return jnp.where(lanes < tq, x[:half], x[half:])


def _split_key(key):
    return (key >> 16).astype(jnp.int16), ((key & 0xFFFF) + I16_MIN).astype(jnp.int16)


def _count16(mask):
    ones = jnp.where(mask, jnp.bfloat16(1), jnp.bfloat16(0)).reshape(-1, COUNT_ROWS, LANES)
    c = ones[0]
    for i in range(1, ones.shape[0]):
        c = c + ones[i]
    return c


def _kth_largest_key(hi_ref, lo_ref, nkt, topk, tq):
    assert hi_ref.shape[1] // COUNT_ROWS * hi_ref.shape[0] <= MAX_EXACT_BF16_COUNT

    def total(count_tile):
        c = lax.fori_loop(0, nkt, lambda kt, c: c + count_tile(kt), jnp.zeros((COUNT_ROWS, LANES), BF16))
        cnt = jnp.sum(c.astype(F32), axis=0, keepdims=True)
        if tq < LANES:
            cnt = cnt + pltpu.roll(cnt, tq, 1)
        return cnt

    def search(ref, base):
        def bit_body(bi, t):
            cand = t | jnp.left_shift(jnp.int32(1), 15 - bi)
            c16 = (cand + I16_MIN).astype(jnp.int16)
            cnt = base + total(lambda kt: _count16(ref[kt] >= c16))
            return jnp.where(cnt >= topk, cand, t)

        return lax.fori_loop(0, 16, bit_body, jnp.zeros((1, LANES), jnp.int32))

    t_hi = search(hi_ref, 0.0) + I16_MIN
    t_hi16 = t_hi.astype(jnp.int16)
    above = total(lambda kt: _count16(hi_ref[kt] > t_hi16))

    def mask_lo(kt, carry):
        lo_ref[kt] = jnp.where(hi_ref[kt] == t_hi16, lo_ref[kt], jnp.int16(I16_MIN))
        return carry

    lax.fori_loop(0, nkt, mask_lo, 0)
    t_lo = search(lo_ref, above)
    return jnp.left_shift(t_hi, 16) | t_lo


def _select_bias(keys, thr, adm):
    return jnp.where(jnp.logical_and(keys >= thr, adm), 0.0, MASK_BIAS)


def _group_q_rows(q_ref, g):
    return jnp.concatenate([q_ref[:, (g * GROUP + r) * HEAD_DIM:(g * GROUP + r + 1) * HEAD_DIM]
                            for r in range(GROUP)], axis=0)


def _attend_groups(ms, ls, bias, k_of, v_of, q_ref, update_acc):
    tq = q_ref.shape[0]
    bias_rep = jnp.concatenate([bias] * (GROUP * tq // LANES), axis=1)

    def scores(g):
        return _dot_nt(k_of(g), _group_q_rows(q_ref, g)) + bias_rep

    new_m, new_l = [], []
    s_next = scores(0)
    for g in range(N_KV_HEADS):
        s = s_next
        if g + 1 < N_KV_HEADS:
            s_next = scores(g + 1)
        m, l, alpha, pv = _attend_tile(ms[g], ls[g], s, v_of(g))
        update_acc(g, alpha, pv)
        new_m.append(m)
        new_l.append(l)
    return tuple(new_m), tuple(new_l)


def _attend_tile(m, l, s, v_t):
    m_new = jnp.maximum(m, jnp.max(s, axis=0, keepdims=True))
    p = jnp.exp2(s - m_new)
    alpha = jnp.exp2(m - m_new)
    l_new = alpha * l + jnp.sum(p, axis=0, keepdims=True)
    pv = lax.dot_general(v_t, p.astype(BF16), (((0,), (0,)), ((), ())), preferred_element_type=F32)
    return m_new, l_new, alpha, pv


def _attend_init(tq):
    n = GROUP * tq
    return (tuple(jnp.full((1, n), MAX_INIT, F32) for _ in range(N_KV_HEADS)),
            tuple(jnp.zeros((1, n), F32) for _ in range(N_KV_HEADS)))


def _write_group(o_ref, g, l, acc, tq):
    o = (acc / l).T
    for r in range(GROUP):
        hs = slice((g * GROUP + r) * HEAD_DIM, (g * GROUP + r + 1) * HEAD_DIM)
        o_ref[:, hs] = o[r * tq:(r + 1) * tq].astype(o_ref.dtype)


def _dsa_prompt_kernel(q_ref, iq_ref, w_ref, ik_ref, k_ref, v_ref, o_ref,
                       key_ref, hi_ref, lo_ref, bias_ref, acc_ref, *, tq, tk, lk, topk):
    qpos_first = pl.program_id(1) * tq
    kend = jnp.minimum(((qpos_first + tq - 1) // CHUNK + 1) * CHUNK, lk)
    nkt = (kend + tk - 1) // tk

    def adm(kt):
        return _admissible(kt * tk, tk, qpos_first, tq)

    def key_rows(kt):
        return pl.ds(pl.multiple_of(kt * tk, tk), tk)

    iq_rows = _stack_heads(iq_ref[...], 0, N_IDX_HEADS, IDX_DIM)
    w_rows = w_ref[0]

    def idx_body(kt, carry):
        keys = _index_keys(ik_ref[0, key_rows(kt), :], iq_rows, w_rows, adm(kt), tq)
        key_ref[kt] = keys
        hi_ref[kt], lo_ref[kt] = _split_key(keys)
        return carry

    lax.fori_loop(0, nkt, idx_body, 0)
    thr = _kth_largest_key(hi_ref, lo_ref, nkt, topk, tq)

    def bias_body(kt, carry):
        bias_ref[kt] = _select_bias(key_ref[kt], thr, adm(kt))
        return carry

    lax.fori_loop(0, nkt, bias_body, 0)

    acc_ref[...] = jnp.zeros(acc_ref.shape, F32)

    def update_acc(g, alpha, pv):
        acc_ref[g] = alpha * acc_ref[g] + pv

    def att_body(kt, carry):
        ks = key_rows(kt)
        return _attend_groups(*carry, bias_ref[kt],
                              lambda g: k_ref[0, ks, g * HEAD_DIM:(g + 1) * HEAD_DIM],
                              lambda g: v_ref[0, ks, g * HEAD_DIM:(g + 1) * HEAD_DIM], q_ref, update_acc)

    _, ls = lax.fori_loop(0, nkt, att_body, _attend_init(tq))
    for g in range(N_KV_HEADS):
        _write_group(o_ref, g, ls[g], acc_ref[g], tq)


def _dsa_prompt_call(qiq, w_rows, ik, k, v, topk, tq, tk):
    n_seq, lk = ik.shape[:2]
    assert lk % tk == 0 and lk % tq == 0 and tq == LANES
    nqb = lk // tq
    qrow = lambda b, i: (b * nqb + i, 0)
    seq = lambda b, i: (b, 0, 0)
    return pl.pallas_call(
        functools.partial(_dsa_prompt_kernel, tq=tq, tk=tk, lk=lk, topk=topk),
        grid=(n_seq, nqb),
        in_specs=[pl.BlockSpec((tq, Q_WIDTH), qrow),
                  pl.BlockSpec((tq, IQ_WIDTH), lambda b, i: (b * nqb + i, Q_WIDTH // IQ_WIDTH)),
                  pl.BlockSpec((1, N_IDX_HEADS, tq), lambda b, i: (b, 0, i)),
                  pl.BlockSpec((1, lk, IDX_DIM), seq),
                  pl.BlockSpec((1, lk, KV_WIDTH), seq),
                  pl.BlockSpec((1, lk, KV_WIDTH), seq)],
        out_specs=pl.BlockSpec((tq, Q_WIDTH), qrow),
        out_shape=jax.ShapeDtypeStruct((n_seq * lk, Q_WIDTH), BF16),
        scratch_shapes=[pltpu.VMEM((lk // tk, tk, LANES), jnp.int32),
                        pltpu.VMEM((lk // tk, tk, LANES), jnp.int16), pltpu.VMEM((lk // tk, tk, LANES), jnp.int16),
                        pltpu.VMEM((lk // tk, tk, LANES), F32),
                        pltpu.VMEM((N_KV_HEADS, HEAD_DIM, GROUP * tq), F32)],
        compiler_params=_params("arbitrary", "arbitrary"),
        name="dsa_prompt",
    )(qiq, qiq, w_rows, ik, k, v)


def _dsa_sample_kernel(q_ref, iq_ref, w_ref, cik_ref, ck_ref, cv_ref, nik_ref, nk_ref, nv_ref, o_ref,
                       key_ref, keyn_ref, hi_ref, lo_ref, *, tq, tk, past, topk):
    nct = past // tk
    iq_rows = _stack_heads(iq_ref[...], 0, N_IDX_HEADS, IDX_DIM)
    w_rows = w_ref[0]

    def adm_cache(kt):
        return _admissible(kt * tk, tk, past, tq)

    adm_new = _admissible(past, tq, past, tq)

    for kt in range(nct):
        ik_t = cik_ref[0, kt * tk:(kt + 1) * tk, :].astype(BF16)
        keys = _index_keys(ik_t, iq_rows, w_rows, adm_cache(kt), tq)
        key_ref[kt] = keys
        hi_ref[kt], lo_ref[kt] = _split_key(_fold_rows(keys, tq))
    keys = _index_keys(nik_ref[...], iq_rows, w_rows, adm_new, tq)
    keyn_ref[...] = keys
    pad = jnp.full(((tk - tq) // 2, LANES), INT_MIN, jnp.int32)
    hi_ref[nct], lo_ref[nct] = _split_key(jnp.concatenate([_fold_rows(keys, tq), pad], axis=0))
    thr = _kth_largest_key(hi_ref, lo_ref, nct + 1, topk, tq)

    carry = _attend_init(tq)
    accs = [jnp.zeros((HEAD_DIM, GROUP * tq), F32) for _ in range(N_KV_HEADS)]

    def update_acc(g, alpha, pv):
        accs[g] = alpha * accs[g] + pv

    for kt in range(nct):
        def cache_rows(ref, g, kt=kt):
            return ref[0, pl.ds(kt * tk * N_KV_HEADS + g, tk, stride=N_KV_HEADS), :].astype(BF16)

        carry = _attend_groups(*carry, _select_bias(key_ref[kt], thr, adm_cache(kt)),
                               functools.partial(cache_rows, ck_ref), functools.partial(cache_rows, cv_ref),
                               q_ref, update_acc)
    carry = _attend_groups(*carry, _select_bias(keyn_ref[...], thr, adm_new),
                           lambda g: nk_ref[:, g * HEAD_DIM:(g + 1) * HEAD_DIM],
                           lambda g: nv_ref[:, g * HEAD_DIM:(g + 1) * HEAD_DIM], q_ref, update_acc)
    for g in range(N_KV_HEADS):
        _write_group(o_ref, g, carry[1][g], accs[g], tq)


def _dsa_sample_call(qiq, w_rows, cache_ik, cache_k, cache_v, new_ik, new_k, new_v, topk, tk):
    n_seq, past = cache_ik.shape[:2]
    tq = new_ik.shape[0] // n_seq
    assert past % tk == 0 and 2 * tq == LANES and (tk // 2) % COUNT_ROWS == 0
    qrow = lambda b: (b, 0)
    seq = lambda b: (b, 0, 0)
    return pl.pallas_call(
        functools.partial(_dsa_sample_kernel, tq=tq, tk=tk, past=past, topk=topk),
        grid=(n_seq,),
        in_specs=[pl.BlockSpec((tq, Q_WIDTH), qrow),
                  pl.BlockSpec((tq, IQ_WIDTH), lambda b: (b, Q_WIDTH // IQ_WIDTH)),
                  pl.BlockSpec((1,) + w_rows.shape[1:], seq),
                  pl.BlockSpec((1, past, IDX_DIM), seq),
                  pl.BlockSpec((1, past * N_KV_HEADS, HEAD_DIM), seq),
                  pl.BlockSpec((1, past * N_KV_HEADS, HEAD_DIM), seq),
                  pl.BlockSpec((tq, IDX_DIM), qrow),
                  pl.BlockSpec((tq, KV_WIDTH), qrow),
                  pl.BlockSpec((tq, KV_WIDTH), qrow)],
        out_specs=pl.BlockSpec((tq, Q_WIDTH), qrow),
        out_shape=jax.ShapeDtypeStruct((n_seq * tq, Q_WIDTH), BF16),
        scratch_shapes=[pltpu.VMEM((past // tk, tk, LANES), jnp.int32), pltpu.VMEM((tq, LANES), jnp.int32),
                        pltpu.VMEM((past // tk + 1, tk // 2, LANES), jnp.int16),
                        pltpu.VMEM((past // tk + 1, tk // 2, LANES), jnp.int16)],
        compiler_params=_params("arbitrary"),
        name="dsa_sample",
    )(qiq, qiq, w_rows, cache_ik, cache_k, cache_v, new_ik, new_k, new_v)


def _rope_tables(pos, reps):
    half = HEAD_DIM // 2
    inv_freq = ROPE_THETA ** (-jnp.arange(half, dtype=F32) / half)
    ang = pos.astype(F32)[:, None] * inv_freq[None, :]
    cos, sin = jnp.cos(ang), jnp.sin(ang)
    cos2 = jnp.concatenate([cos, cos], axis=1)
    sin2 = jnp.concatenate([-sin, sin], axis=1)
    return jnp.tile(cos2, (reps, 1)), jnp.tile(sin2, (reps, 1))


DEFAULT_CFG = dict(tm=1024, tn_mix=512, tn_ffn=512, chunk=256, tm_ln=512, tk_down=1408, tn_proj=512,
                   tq_p=128, tk_p=512, tk_s=1024)


def _forward(x_prompt, x_sample, state_conv_mix, cache_k, cache_v, cache_idx_k, state_ffn_conv,
             mix_w_in, mix_conv_w, mix_w_out, attn_w_in, idx_k_norm_g, idx_k_norm_b, attn_w_out,
             ffn_w_in, ffn_conv_w, ffn_conv_b, ffn_w_down, ln1_g, ln1_b, ln2_g, ln2_b, cfg):
    bp, tp, d = x_prompt.shape
    bs, ts, _ = x_sample.shape
    past = cache_k.shape[2]
    depth = ln1_g.shape[0]
    d_ff = ffn_conv_b.shape[1]
    alpha = (2 * depth) ** 0.25
    topk_p = min(INDEX_TOPK, tp // 4)
    topk_s = min(INDEX_TOPK, (past + ts) // 4)
    tm = cfg["tm"]
    mix_w_in_b, mix_w_out_b = mix_w_in.astype(BF16), mix_w_out.astype(BF16)
    attn_w_out_b = attn_w_out.astype(BF16)
    ffn_w_in_b, ffn_w_down_b = ffn_w_in.astype(BF16), ffn_w_down.astype(BF16)

    streams = {
        "p": [x_prompt.reshape(bp * tp, d), x_prompt.reshape(bp * tp, d).astype(BF16), tp, bp],
        "s": [x_sample.reshape(bs * ts, d), x_sample.reshape(bs * ts, d).astype(BF16), ts, bs],
    }
    out = {k: {"conv": [], "k": [], "v": [], "ik": [], "ffn": []} for k in streams}

    def tables(name):
        seq_len, n_rows = streams[name][2], streams[name][0].shape[0]
        tile = _row_tile(n_rows, tm)
        if name == "p":
            pos = jnp.arange(seq_len, dtype=jnp.int32)
        else:
            pos = past + jnp.arange(seq_len, dtype=jnp.int32)
        return _rope_tables(pos, max(1, tile // seq_len))

    for i in range(depth):
        j = i // 2
        if i % 2 == 0:
            for name, (xf, xb, seq_len, n_seq) in streams.items():
                st = jnp.zeros((n_seq, 2, d), F32) if name == "p" else state_conv_mix[j]
                z, nst = _gated_conv_call(_mix_in_kernel, "mix_in_" + name, xb, mix_w_in_b, j, 3, mix_conv_w[j],
                                          None, st, seq_len, tm, cfg["tn_mix"], cfg["chunk"])
                out[name]["conv"].append(nst)
                xf, xb = _mm_ln_call(z, mix_w_out_b, j, xf, ln1_g[i], ln1_b[i], alpha, cfg["tm_ln"], d)
                streams[name][0], streams[name][1] = xf, xb
        else:
            w_in = attn_w_in[j].astype(BF16)
            splits = (Q_WIDTH, Q_WIDTH + KV_WIDTH, Q_WIDTH + 2 * KV_WIDTH,
                      Q_WIDTH + 2 * KV_WIDTH + IQ_WIDTH, Q_WIDTH + 2 * KV_WIDTH + IQ_WIDTH + IDX_DIM)
            w_idx = jnp.concatenate(
                [w_in[:, splits[3]:], jnp.zeros((d, IDX_DIM - N_IDX_HEADS), BF16)], axis=1)
            tn = cfg["tn_proj"]
            q_blocks = list(range(0, Q_WIDTH // tn)) + list(range(splits[2] // tn, splits[3] // tn))
            k_blocks = list(range(splits[0] // tn, splits[1] // tn))
            v_blocks = list(range(splits[1] // tn, splits[2] // tn))
            for name, (xf, xb, seq_len, n_seq) in streams.items():
                cos, sin = tables(name)
                qiq, = _proj_call(xb, w_in, q_blocks, cos, sin, True, [BF16], tm, tn, n_scaled=Q_WIDTH // tn)
                k_f, k_b = _proj_call(xb, w_in, k_blocks, cos, sin, True, [F32, BF16], tm, tn)
                v_f, v_b = _proj_call(xb, w_in, v_blocks, cos, sin, False, [F32, BF16], tm, tn)
                ik_f, ik_b, iw = _idx_proj_call(xb, w_idx, cos, sin, idx_k_norm_g[j], idx_k_norm_b[j], tm)
                out[name]["k"].append(k_f.reshape(n_seq, seq_len, N_KV_HEADS, HEAD_DIM))
                out[name]["v"].append(v_f.reshape(n_seq, seq_len, N_KV_HEADS, HEAD_DIM))
                out[name]["ik"].append(ik_f.reshape(n_seq, seq_len, IDX_DIM))
                w_rows = iw[:, :N_IDX_HEADS].reshape(n_seq, seq_len, N_IDX_HEADS).transpose(0, 2, 1)
                if name == "p":
                    o = _dsa_prompt_call(qiq, w_rows, ik_b.reshape(n_seq, seq_len, IDX_DIM),
                                         k_b.reshape(n_seq, seq_len, KV_WIDTH), v_b.reshape(n_seq, seq_len, KV_WIDTH),
                                         topk_p, cfg["tq_p"], min(cfg["tk_p"], seq_len))
                else:
                    w_rows = w_rows.reshape(n_seq, N_IDX_HEADS * seq_len // LANES, LANES)
                    o = _dsa_sample_call(qiq, w_rows, cache_idx_k[j],
                                         cache_k[j].reshape(n_seq, past * N_KV_HEADS, HEAD_DIM),
                                         cache_v[j].reshape(n_seq, past * N_KV_HEADS, HEAD_DIM),
                                         ik_b, k_b, v_b, topk_s, min(cfg["tk_s"], past))
                xf, xb = _mm_ln_call(o, attn_w_out_b, j, xf, ln1_g[i], ln1_b[i], alpha, cfg["tm_ln"], Q_WIDTH)
                streams[name][0], streams[name][1] = xf, xb

        for name, (xf, xb, seq_len, n_seq) in streams.items():
            st = jnp.zeros((n_seq, 2, d_ff), F32) if name == "p" else state_ffn_conv[i]
            h, nst = _gated_conv_call(_ffn_in_kernel, "ffn_in_" + name, xb, ffn_w_in_b, i, 2, ffn_conv_w[i],
                                      ffn_conv_b[i], st, seq_len, tm, cfg["tn_ffn"], cfg["chunk"])
            out[name]["ffn"].append(nst)
            xf, xb = _mm_ln_call(h, ffn_w_down_b, i, xf, ln2_g[i], ln2_b[i], alpha, cfg["tm_ln"], cfg["tk_down"])
            streams[name][0], streams[name][1] = xf, xb

    p, s = out["p"], out["s"]
    return (streams["p"][0].reshape(bp, tp, d), streams["s"][0].reshape(bs, ts, d),
            jnp.stack(p["conv"]), jnp.stack(p["k"]), jnp.stack(p["v"]), jnp.stack(p["ik"]), jnp.stack(p["ffn"]),
            jnp.stack(s["conv"]), jnp.stack(s["k"]), jnp.stack(s["v"]), jnp.stack(s["ik"]), jnp.stack(s["ffn"]))


def kernel(x_prompt, x_sample, state_conv_mix, cache_k, cache_v, cache_idx_k, state_ffn_conv, mix_w_in, mix_conv_w, mix_w_out, attn_w_in, idx_k_norm_g, idx_k_norm_b, attn_w_out, ffn_w_in, ffn_conv_w, ffn_conv_b, ffn_w_down, ln1_g, ln1_b, ln2_g, ln2_b):
    return _forward(x_prompt, x_sample, state_conv_mix, cache_k, cache_v, cache_idx_k, state_ffn_conv,
                    mix_w_in, mix_conv_w, mix_w_out, attn_w_in, idx_k_norm_g, idx_k_norm_b, attn_w_out,
                    ffn_w_in, ffn_conv_w, ffn_conv_b, ffn_w_down, ln1_g, ln1_b, ln2_g, ln2_b, DEFAULT_CFG)
```

```python
import functools

import jax
import jax.numpy as jnp
from jax import lax
from jax.experimental import pallas as pl
from jax.experimental.pallas import tpu as pltpu

CHUNK = 64
CHUNK_SHIFT = CHUNK.bit_length() - 1
assert 1 << CHUNK_SHIFT == CHUNK
N_HEADS = 16
N_KV_HEADS = 4
HEAD_DIM = 128
N_IDX_HEADS = 16
IDX_DIM = 128
INDEX_TOPK = 256
ROPE_THETA = 10000.0
LN_EPS = 1e-5
ATTN_SCALE = HEAD_DIM ** -0.5
IDX_W_SCALE = (N_IDX_HEADS ** -0.5) * (IDX_DIM ** -0.5)
Q_WIDTH = N_HEADS * HEAD_DIM
KV_WIDTH = N_KV_HEADS * HEAD_DIM
IQ_WIDTH = N_IDX_HEADS * IDX_DIM
GROUP = N_HEADS // N_KV_HEADS

LANES = 128
VMEM_LIMIT_BYTES = 56 * 1024 * 1024
MASK_BIAS = -1e30
MAX_INIT = -5e29
INT_MIN = -2 ** 31
LOG2_E = 1.4426950408889634

F32 = jnp.float32
BF16 = jnp.bfloat16


def _dot(a, b):
    return jnp.dot(a, b, preferred_element_type=F32)


def _dot_nt(a, b):
    return lax.dot_general(a, b, (((1,), (1,)), ((), ())), preferred_element_type=F32)


def _params(*sem):
    return pltpu.CompilerParams(dimension_semantics=sem, vmem_limit_bytes=VMEM_LIMIT_BYTES)


def _row_tile(n_rows, want):
    return want if n_rows % want == 0 else n_rows


def _conv3_rows(u, prev, w):
    rows = lax.broadcasted_iota(jnp.int32, u.shape, 0)
    p2, p1 = prev[0:1], prev[1:2]
    u1 = jnp.where(rows == 0, p1, pltpu.roll(u, 1, 0))
    u2 = jnp.where(rows == 0, p2, jnp.where(rows == 1, p1, pltpu.roll(u, 2, 0)))
    return w[0:1] * u2 + w[1:2] * u1 + w[2:3] * u


def _segmented_conv(u, cs, w_ref, st_ref, nst_ref, carry_ref, emit, *, seg, nseg, tiles_per_seq):
    w = w_ref[:, cs]
    if nseg == 1:
        emit(slice(0, seg), _conv3_rows(u, carry_ref[:, cs], w))
        tail = u[seg - 2:seg]
        carry_ref[:, cs] = tail
        nst_ref[0, :, cs] = tail
    else:
        for s in range(nseg):
            rs = slice(s * seg, (s + 1) * seg)
            us = u[rs]
            emit(rs, _conv3_rows(us, st_ref[s, :, cs], w))
            nst_ref[s, :, cs] = us[seg - 2:seg]


def _chunked_gated_conv(matmuls, epilogue, st_ref, carry_ref, n_cols, chunk, seg_kw):
    if seg_kw["nseg"] == 1:
        @pl.when(pl.program_id(1) % seg_kw["tiles_per_seq"] == 0)
        def _():
            carry_ref[...] = st_ref[0]

    chunks = [slice(c, c + chunk) for c in range(0, n_cols, chunk)]
    nxt = matmuls(chunks[0])
    for c, cs in enumerate(chunks):
        cur = nxt
        if c + 1 < len(chunks):
            nxt = matmuls(chunks[c + 1])
        epilogue(cs, cur)


def _mix_in_kernel(x_ref, wb_ref, wc_ref, wv_ref, cw_ref, st_ref, z_ref, nst_ref, carry_ref, *, chunk, **seg_kw):
    x = x_ref[...]

    def matmuls(cs):
        return _dot(x, wc_ref[:, cs]) * _dot(x, wv_ref[:, cs]), _dot(x, wb_ref[:, cs])

    def epilogue(cs, res):
        u, b = res

        def emit(rs, conv):
            z_ref[rs, cs] = (b[rs] * conv).astype(z_ref.dtype)

        _segmented_conv(u, cs, cw_ref, st_ref, nst_ref, carry_ref, emit, **seg_kw)

    _chunked_gated_conv(matmuls, epilogue, st_ref, carry_ref, z_ref.shape[1], chunk, seg_kw)


def _ffn_in_kernel(x_ref, wg_ref, wu_ref, cw_ref, cb_ref, st_ref, h_ref, nst_ref, carry_ref, *, chunk, **seg_kw):
    x = x_ref[...]

    def matmuls(cs):
        return _dot(x, wg_ref[:, cs]), _dot(x, wu_ref[:, cs])

    def epilogue(cs, res):
        g, up = res
        cb = cb_ref[:, cs]

        def emit(rs, conv):
            a = conv + cb
            h_ref[rs, cs] = (a * jax.nn.sigmoid(a) * up[rs]).astype(h_ref.dtype)

        _segmented_conv(g, cs, cw_ref, st_ref, nst_ref, carry_ref, emit, **seg_kw)

    _chunked_gated_conv(matmuls, epilogue, st_ref, carry_ref, h_ref.shape[1], chunk, seg_kw)


def _gated_conv_call(body, name, x, w, layer, n_branches, conv_w, conv_b, state, seq_len, tm, tn, chunk):
    n, k = x.shape
    c = w.shape[2] // n_branches
    tm = _row_tile(n, tm)
    tn = tn if c % tn == 0 else c
    nct, nrt = c // tn, n // tm
    if seq_len >= tm:
        assert seq_len % tm == 0
        seg, nseg, tiles_per_seq = tm, 1, seq_len // tm
        st_spec = pl.BlockSpec((1, 2, tn), lambda j, i: (i // tiles_per_seq, 0, j))
    else:
        assert tm % seq_len == 0
        seg, nseg, tiles_per_seq = seq_len, tm // seq_len, 1
        st_spec = pl.BlockSpec((nseg, 2, tn), lambda j, i: (i, 0, j))
    chunk = chunk if (tn % chunk == 0 and nseg == 1) else tn
    w_specs = [pl.BlockSpec((None, k, tn), functools.partial(lambda j, i, o: (layer, 0, j + o), o=b * nct))
               for b in range(n_branches)]
    vec_specs = [pl.BlockSpec((conv_w.shape[0], tn), lambda j, i: (0, j))]
    vec_args = [conv_w]
    if conv_b is not None:
        vec_specs.append(pl.BlockSpec((1, tn), lambda j, i: (0, j)))
        vec_args.append(conv_b.reshape(1, c))
    def with_cast_weights(*refs):
        w_f32, rest, w_bf16 = refs[1:1 + n_branches], refs[1 + n_branches:-n_branches], refs[-n_branches:]

        @pl.when(pl.program_id(1) == 0)
        def _():
            for src, dst in zip(w_f32, w_bf16):
                dst[...] = src[...].astype(dst.dtype)

        body(refs[0], *w_bf16, *rest, chunk=chunk, seg=seg, nseg=nseg, tiles_per_seq=tiles_per_seq)

    return pl.pallas_call(
        with_cast_weights,
        grid=(nct, nrt),
        in_specs=[pl.BlockSpec((tm, k), lambda j, i: (i, 0))] + w_specs + vec_specs + [st_spec],
        out_specs=[pl.BlockSpec((tm, tn), lambda j, i: (i, j)), st_spec],
        out_shape=[jax.ShapeDtypeStruct((n, c), BF16), jax.ShapeDtypeStruct(state.shape, F32)],
        scratch_shapes=[pltpu.VMEM((2, tn), F32)] + [pltpu.VMEM((k, tn), BF16)] * n_branches,
        compiler_params=_params("arbitrary", "arbitrary"),
        name=name,
    )(x, *([w] * n_branches), *vec_args, state)


def _mm_ln_kernel(a_ref, w_ref, x_ref, g_ref, b_ref, o_ref, ob_ref, *acc, alpha, nk):
    def finish(y):
        r = alpha * x_ref[...] + y
        mu = jnp.mean(r, axis=-1, keepdims=True)
        rc = r - mu
        var = jnp.mean(rc * rc, axis=-1, keepdims=True)
        out = rc * lax.rsqrt(var + LN_EPS) * g_ref[...] + b_ref[...]
        o_ref[...] = out
        ob_ref[...] = out.astype(ob_ref.dtype)

    part = _dot(a_ref[...], w_ref[...])
    if nk == 1:
        finish(part)
    else:
        acc_ref, = acc
        kk = pl.program_id(1)

        @pl.when(kk == 0)
        def _():
            acc_ref[...] = part

        @pl.when(jnp.logical_and(kk > 0, kk < nk - 1))
        def _():
            acc_ref[...] += part

        @pl.when(kk == nk - 1)
        def _():
            finish(acc_ref[...] + part)


def _mm_ln_call(a, w, layer, x, g, b, alpha, tm, tk):
    n, k = a.shape
    d = w.shape[2]
    tm = _row_tile(n, tm)
    tk = tk if k % tk == 0 else k
    nk = k // tk
    return pl.pallas_call(
        functools.partial(_mm_ln_kernel, alpha=alpha, nk=nk),
        grid=(n // tm, nk),
        in_specs=[pl.BlockSpec((tm, tk), lambda i, kk: (i, kk)),
                  pl.BlockSpec((None, tk, d), lambda i, kk: (layer, kk, 0)),
                  pl.BlockSpec((tm, d), lambda i, kk: (i, 0)),
                  pl.BlockSpec((1, d), lambda i, kk: (0, 0)),
                  pl.BlockSpec((1, d), lambda i, kk: (0, 0))],
        out_specs=[pl.BlockSpec((tm, d), lambda i, kk: (i, 0)),
                   pl.BlockSpec((tm, d), lambda i, kk: (i, 0))],
        out_shape=[jax.ShapeDtypeStruct((n, d), F32), jax.ShapeDtypeStruct((n, d), BF16)],
        scratch_shapes=[pltpu.VMEM((tm, d), F32)] if nk > 1 else [],
        compiler_params=_params("arbitrary", "arbitrary"),
        name=f"mm_ln_k{k}_rows{n}",
    )(a, w, x, g.reshape(1, d), b.reshape(1, d))


def _rope_head(y, cos, sin):
    return y * cos + pltpu.roll(y, HEAD_DIM // 2, 1) * sin


def _proj_kernel(x_ref, w_ref, cos_ref, sin_ref, *outs, rope, n_scaled):
    y = _dot(x_ref[...], w_ref[...])
    cos, sin = cos_ref[...], sin_ref[...]
    if n_scaled:
        scale = jnp.where(pl.program_id(0) < n_scaled, ATTN_SCALE * LOG2_E, 1.0)
    for h in range(y.shape[1] // HEAD_DIM):
        hs = slice(h * HEAD_DIM, (h + 1) * HEAD_DIM)
        yh = y[:, hs]
        if rope:
            yh = _rope_head(yh, cos, sin)
        if n_scaled:
            yh = yh * scale
        for o_ref in outs:
            o_ref[:, hs] = yh.astype(o_ref.dtype)


def _proj_call(x, w, col_blocks, cos, sin, rope, out_dtypes, tm, tn, n_scaled=0):
    n, k = x.shape
    tm = _row_tile(n, tm)
    nrt = n // tm
    tab_tiles = cos.shape[0] // tm
    blocks = jnp.asarray(col_blocks, jnp.int32)
    width = len(col_blocks) * tn
    grid_spec = pltpu.PrefetchScalarGridSpec(
        num_scalar_prefetch=1,
        grid=(len(col_blocks), nrt),
        in_specs=[pl.BlockSpec((tm, k), lambda j, i, blk: (i, 0)),
                  pl.BlockSpec((k, tn), lambda j, i, blk: (0, blk[j])),
                  pl.BlockSpec((tm, HEAD_DIM), lambda j, i, blk: (i % tab_tiles, 0)),
                  pl.BlockSpec((tm, HEAD_DIM), lambda j, i, blk: (i % tab_tiles, 0))],
        out_specs=[pl.BlockSpec((tm, tn), lambda j, i, blk: (i, j)) for _ in out_dtypes],
    )

    def body(blk_ref, *refs):
        del blk_ref
        _proj_kernel(*refs, rope=rope, n_scaled=n_scaled)

    return pl.pallas_call(
        body,
        grid_spec=grid_spec,
        out_shape=[jax.ShapeDtypeStruct((n, width), dt) for dt in out_dtypes],
        compiler_params=_params("arbitrary", "arbitrary"),
        name=f"proj_{len(col_blocks)}blk_rope{int(rope)}_rows{n}",
    )(blocks, x, w, cos, sin)


def _idx_proj_kernel(x_ref, w_ref, cos_ref, sin_ref, g_ref, b_ref, ik_ref, ikb_ref, iw_ref):
    y = _dot(x_ref[...], w_ref[...])
    a = y[:, :IDX_DIM]
    mu = jnp.mean(a, axis=-1, keepdims=True)
    ac = a - mu
    var = jnp.mean(ac * ac, axis=-1, keepdims=True)
    a = ac * lax.rsqrt(var + LN_EPS) * g_ref[...] + b_ref[...]
    a = _rope_head(a, cos_ref[...], sin_ref[...])
    ik_ref[...] = a
    ikb_ref[...] = a.astype(ikb_ref.dtype)
    iw_ref[...] = y[:, IDX_DIM:] * IDX_W_SCALE


def _idx_proj_call(x, w, cos, sin, g, b, tm):
    n, k = x.shape
    tm = _row_tile(n, tm)
    tab_tiles = cos.shape[0] // tm
    row = lambda i: (i, 0)
    tab = lambda i: (i % tab_tiles, 0)
    fixed = lambda i: (0, 0)
    return pl.pallas_call(
        _idx_proj_kernel,
        grid=(n // tm,),
        in_specs=[pl.BlockSpec((tm, k), row), pl.BlockSpec((k, 2 * IDX_DIM), fixed),
                  pl.BlockSpec((tm, IDX_DIM), tab), pl.BlockSpec((tm, IDX_DIM), tab),
                  pl.BlockSpec((1, IDX_DIM), fixed), pl.BlockSpec((1, IDX_DIM), fixed)],
        out_specs=[pl.BlockSpec((tm, IDX_DIM), row)] * 3,
        out_shape=[jax.ShapeDtypeStruct((n, IDX_DIM), F32), jax.ShapeDtypeStruct((n, IDX_DIM), BF16),
                   jax.ShapeDtypeStruct((n, IDX_DIM), F32)],
        compiler_params=_params("arbitrary"),
        name=f"idx_proj_rows{n}",
    )(x, w, cos, sin, g.reshape(1, IDX_DIM), b.reshape(1, IDX_DIM))


def _stack_heads(x, first, count, width):
    return jnp.concatenate([x[:, (first + r) * width:(first + r + 1) * width] for r in range(count)], axis=0)


def _admissible(kpos_first, n_keys, qpos_first, tq):
    rows = lax.broadcasted_iota(jnp.int32, (n_keys, LANES), 0)
    lanes = lax.broadcasted_iota(jnp.int32, (n_keys, LANES), 1)
    return ((kpos_first + rows) >> CHUNK_SHIFT) <= ((qpos_first + (lanes & (tq - 1))) >> CHUNK_SHIFT)


def _index_keys(ik_t, iq_rows, w_rows, adm, tq):
    s = _dot_nt(ik_t, iq_rows)
    acc = jnp.maximum(s[:, :LANES], 0.0) * w_rows[0:1]
    for j in range(1, s.shape[1] // LANES):
        acc = acc + jnp.maximum(s[:, j * LANES:(j + 1) * LANES], 0.0) * w_rows[j:j + 1]
    if tq < LANES:
        acc = acc + pltpu.roll(acc, tq, 1)
    acc = jnp.where(adm, acc, -jnp.inf)
    bits = pltpu.bitcast(acc, jnp.int32)
    return bits ^ ((bits >> 31) & 0x7FFFFFFF)


COUNT_ROWS = 64
MAX_EXACT_BF16_COUNT = 256
I16_MIN = -2 ** 15


def _fold_rows(x, tq):
    half = x.shape[0] // 2
    lanes = lax.broadcasted_iota(jnp.int32, (half, LANES), 1)
    return jnp.where(lanes < tq, x[:half], x[half:])


def _split_key(key):
    return (key >> 16).astype(jnp.int16), ((key & 0xFFFF) + I16_MIN).astype(jnp.int16)


def _count16(mask):
    ones = jnp.where(mask, jnp.bfloat16(1), jnp.bfloat16(0)).reshape(-1, COUNT_ROWS, LANES)
    c = ones[0]
    for i in range(1, ones.shape[0]):
        c = c + ones[i]
    return c


def _kth_largest_key(hi_ref, lo_ref, nkt, topk, tq):
    assert hi_ref.shape[1] // COUNT_ROWS * hi_ref.shape[0] <= MAX_EXACT_BF16_COUNT

    def total(count_tile):
        c = lax.fori_loop(0, nkt, lambda kt, c: c + count_tile(kt), jnp.zeros((COUNT_ROWS, LANES), BF16))
        cnt = jnp.sum(c.astype(F32), axis=0, keepdims=True)
        if tq < LANES:
            cnt = cnt + pltpu.roll(cnt, tq, 1)
        return cnt

    def search(ref, base):
        def bit_body(bi, t):
            cand = t | jnp.left_shift(jnp.int32(1), 15 - bi)
            c16 = (cand + I16_MIN).astype(jnp.int16)
            cnt = base + total(lambda kt: _count16(ref[kt] >= c16))
            return jnp.where(cnt >= topk, cand, t)

        return lax.fori_loop(0, 16, bit_body, jnp.zeros((1, LANES), jnp.int32))

    t_hi = search(hi_ref, 0.0) + I16_MIN
    t_hi16 = t_hi.astype(jnp.int16)
    above = total(lambda kt: _count16(hi_ref[kt] > t_hi16))

    def mask_lo(kt, carry):
        lo_ref[kt] = jnp.where(hi_ref[kt] == t_hi16, lo_ref[kt], jnp.int16(I16_MIN))
        return carry

    lax.fori_loop(0, nkt, mask_lo, 0)
    t_lo = search(lo_ref, above)
    return jnp.left_shift(t_hi, 16) | t_lo


def _select_bias(keys, thr, adm):
    return jnp.where(jnp.logical_and(keys >= thr, adm), 0.0, MASK_BIAS)


def _group_q_rows(q_ref, g):
    return jnp.concatenate([q_ref[:, (g * GROUP + r) * HEAD_DIM:(g * GROUP + r + 1) * HEAD_DIM]
                            for r in range(GROUP)], axis=0)


def _attend_groups(ms, ls, bias, k_of, v_of, q_ref, update_acc):
    tq = q_ref.shape[0]
    bias_rep = jnp.concatenate([bias] * (GROUP * tq // LANES), axis=1)

    def scores(g):
        return _dot_nt(k_of(g), _group_q_rows(q_ref, g)) + bias_rep

    new_m, new_l = [], []
    s_next = scores(0)
    for g in range(N_KV_HEADS):
        s = s_next
        if g + 1 < N_KV_HEADS:
            s_next = scores(g + 1)
        m, l, alpha, pv = _attend_tile(ms[g], ls[g], s, v_of(g))
        update_acc(g, alpha, pv)
        new_m.append(m)
        new_l.append(l)
    return tuple(new_m), tuple(new_l)


def _attend_tile(m, l, s, v_t):
    m_new = jnp.maximum(m, jnp.max(s, axis=0, keepdims=True))
    p = jnp.exp2(s - m_new)
    alpha = jnp.exp2(m - m_new)
    l_new = alpha * l + jnp.sum(p, axis=0, keepdims=True)
    pv = lax.dot_general(v_t, p.astype(BF16), (((0,), (0,)), ((), ())), preferred_element_type=F32)
    return m_new, l_new, alpha, pv


def _attend_init(tq):
    n = GROUP * tq
    return (tuple(jnp.full((1, n), MAX_INIT, F32) for _ in range(N_KV_HEADS)),
            tuple(jnp.zeros((1, n), F32) for _ in range(N_KV_HEADS)))


def _write_group(o_ref, g, l, acc, tq):
    o = (acc / l).T
    for r in range(GROUP):
        hs = slice((g * GROUP + r) * HEAD_DIM, (g * GROUP + r + 1) * HEAD_DIM)
        o_ref[:, hs] = o[r * tq:(r + 1) * tq].astype(o_ref.dtype)


def _dsa_prompt_kernel(q_ref, iq_ref, w_ref, ik_ref, k_ref, v_ref, o_ref,
                       key_ref, hi_ref, lo_ref, bias_ref, acc_ref, *, tq, tk, lk, topk):
    qpos_first = pl.program_id(1) * tq
    kend = jnp.minimum(((qpos_first + tq - 1) // CHUNK + 1) * CHUNK, lk)
    nkt = (kend + tk - 1) // tk

    def adm(kt):
        return _admissible(kt * tk, tk, qpos_first, tq)

    def key_rows(kt):
        return pl.ds(pl.multiple_of(kt * tk, tk), tk)

    iq_rows = _stack_heads(iq_ref[...], 0, N_IDX_HEADS, IDX_DIM)
    w_rows = w_ref[0]

    def idx_body(kt, carry):
        keys = _index_keys(ik_ref[0, key_rows(kt), :], iq_rows, w_rows, adm(kt), tq)
        key_ref[kt] = keys
        hi_ref[kt], lo_ref[kt] = _split_key(keys)
        return carry

    lax.fori_loop(0, nkt, idx_body, 0)
    thr = _kth_largest_key(hi_ref, lo_ref, nkt, topk, tq)

    def bias_body(kt, carry):
        bias_ref[kt] = _select_bias(key_ref[kt], thr, adm(kt))
        return carry

    lax.fori_loop(0, nkt, bias_body, 0)

    acc_ref[...] = jnp.zeros(acc_ref.shape, F32)

    def update_acc(g, alpha, pv):
        acc_ref[g] = alpha * acc_ref[g] + pv

    def att_body(kt, carry):
        ks = key_rows(kt)
        return _attend_groups(*carry, bias_ref[kt],
                              lambda g: k_ref[0, ks, g * HEAD_DIM:(g + 1) * HEAD_DIM],
                              lambda g: v_ref[0, ks, g * HEAD_DIM:(g + 1) * HEAD_DIM], q_ref, update_acc)

    _, ls = lax.fori_loop(0, nkt, att_body, _attend_init(tq))
    for g in range(N_KV_HEADS):
        _write_group(o_ref, g, ls[g], acc_ref[g], tq)


def _dsa_prompt_call(qiq, w_rows, ik, k, v, topk, tq, tk):
    n_seq, lk = ik.shape[:2]
    assert lk % tk == 0 and lk % tq == 0 and tq == LANES
    nqb = lk // tq
    qrow = lambda b, i: (b * nqb + i, 0)
    seq = lambda b, i: (b, 0, 0)
    return pl.pallas_call(
        functools.partial(_dsa_prompt_kernel, tq=tq, tk=tk, lk=lk, topk=topk),
        grid=(n_seq, nqb),
        in_specs=[pl.BlockSpec((tq, Q_WIDTH), qrow),
                  pl.BlockSpec((tq, IQ_WIDTH), lambda b, i: (b * nqb + i, Q_WIDTH // IQ_WIDTH)),
                  pl.BlockSpec((1, N_IDX_HEADS, tq), lambda b, i: (b, 0, i)),
                  pl.BlockSpec((1, lk, IDX_DIM), seq),
                  pl.BlockSpec((1, lk, KV_WIDTH), seq),
                  pl.BlockSpec((1, lk, KV_WIDTH), seq)],
        out_specs=pl.BlockSpec((tq, Q_WIDTH), qrow),
        out_shape=jax.ShapeDtypeStruct((n_seq * lk, Q_WIDTH), BF16),
        scratch_shapes=[pltpu.VMEM((lk // tk, tk, LANES), jnp.int32),
                        pltpu.VMEM((lk // tk, tk, LANES), jnp.int16), pltpu.VMEM((lk // tk, tk, LANES), jnp.int16),
                        pltpu.VMEM((lk // tk, tk, LANES), F32),
                        pltpu.VMEM((N_KV_HEADS, HEAD_DIM, GROUP * tq), F32)],
        compiler_params=_params("arbitrary", "arbitrary"),
        name="dsa_prompt",
    )(qiq, qiq, w_rows, ik, k, v)


def _dsa_sample_kernel(q_ref, iq_ref, w_ref, cik_ref, ck_ref, cv_ref, nik_ref, nk_ref, nv_ref, o_ref,
                       key_ref, keyn_ref, hi_ref, lo_ref, *, tq, tk, past, topk):
    nct = past // tk
    iq_rows = _stack_heads(iq_ref[...], 0, N_IDX_HEADS, IDX_DIM)
    w_rows = w_ref[0]

    def adm_cache(kt):
        return _admissible(kt * tk, tk, past, tq)

    adm_new = _admissible(past, tq, past, tq)

    for kt in range(nct):
        ik_t = cik_ref[0, kt * tk:(kt + 1) * tk, :].astype(BF16)
        keys = _index_keys(ik_t, iq_rows, w_rows, adm_cache(kt), tq)
        key_ref[kt] = keys
        hi_ref[kt], lo_ref[kt] = _split_key(_fold_rows(keys, tq))
    keys = _index_keys(nik_ref[...], iq_rows, w_rows, adm_new, tq)
    keyn_ref[...] = keys
    pad = jnp.full(((tk - tq) // 2, LANES), INT_MIN, jnp.int32)
    hi_ref[nct], lo_ref[nct] = _split_key(jnp.concatenate([_fold_rows(keys, tq), pad], axis=0))
    thr = _kth_largest_key(hi_ref, lo_ref, nct + 1, topk, tq)

    carry = _attend_init(tq)
    accs = [jnp.zeros((HEAD_DIM, GROUP * tq), F32) for _ in range(N_KV_HEADS)]

    def update_acc(g, alpha, pv):
        accs[g] = alpha * accs[g] + pv

    for kt in range(nct):
        def cache_rows(ref, g, kt=kt):
            return ref[0, pl.ds(kt * tk * N_KV_HEADS + g, tk, stride=N_KV_HEADS), :].astype(BF16)

        carry = _attend_groups(*carry, _select_bias(key_ref[kt], thr, adm_cache(kt)),
                               functools.partial(cache_rows, ck_ref), functools.partial(cache_rows, cv_ref),
                               q_ref, update_acc)
    carry = _attend_groups(*carry, _select_bias(keyn_ref[...], thr, adm_new),
                           lambda g: nk_ref[:, g * HEAD_DIM:(g + 1) * HEAD_DIM],
                           lambda g: nv_ref[:, g * HEAD_DIM:(g + 1) * HEAD_DIM], q_ref, update_acc)
    for g in range(N_KV_HEADS):
        _write_group(o_ref, g, carry[1][g], accs[g], tq)


def _dsa_sample_call(qiq, w_rows, cache_ik, cache_k, cache_v, new_ik, new_k, new_v, topk, tk):
    n_seq, past = cache_ik.shape[:2]
    tq = new_ik.shape[0] // n_seq
    assert past % tk == 0 and 2 * tq == LANES and (tk // 2) % COUNT_ROWS == 0
    qrow = lambda b: (b, 0)
    seq = lambda b: (b, 0, 0)
    return pl.pallas_call(
        functools.partial(_dsa_sample_kernel, tq=tq, tk=tk, past=past, topk=topk),
        grid=(n_seq,),
        in_specs=[pl.BlockSpec((tq, Q_WIDTH), qrow),
                  pl.BlockSpec((tq, IQ_WIDTH), lambda b: (b, Q_WIDTH // IQ_WIDTH)),
                  pl.BlockSpec((1,) + w_rows.shape[1:], seq),
                  pl.BlockSpec((1, past, IDX_DIM), seq),
                  pl.BlockSpec((1, past * N_KV_HEADS, HEAD_DIM), seq),
                  pl.BlockSpec((1, past * N_KV_HEADS, HEAD_DIM), seq),
                  pl.BlockSpec((tq, IDX_DIM), qrow),
                  pl.BlockSpec((tq, KV_WIDTH), qrow),
                  pl.BlockSpec((tq, KV_WIDTH), qrow)],
        out_specs=pl.BlockSpec((tq, Q_WIDTH), qrow),
        out_shape=jax.ShapeDtypeStruct((n_seq * tq, Q_WIDTH), BF16),
        scratch_shapes=[pltpu.VMEM((past // tk, tk, LANES), jnp.int32), pltpu.VMEM((tq, LANES), jnp.int32),
                        pltpu.VMEM((past // tk + 1, tk // 2, LANES), jnp.int16),
                        pltpu.VMEM((past // tk + 1, tk // 2, LANES), jnp.int16)],
        compiler_params=_params("arbitrary"),
        name="dsa_sample",
    )(qiq, qiq, w_rows, cache_ik, cache_k, cache_v, new_ik, new_k, new_v)


def _rope_tables(pos, reps):
    half = HEAD_DIM // 2
    inv_freq = ROPE_THETA ** (-jnp.arange(half, dtype=F32) / half)
    ang = pos.astype(F32)[:, None] * inv_freq[None, :]
    cos, sin = jnp.cos(ang), jnp.sin(ang)
    cos2 = jnp.concatenate([cos, cos], axis=1)
    sin2 = jnp.concatenate([-sin, sin], axis=1)
    return jnp.tile(cos2, (reps, 1)), jnp.tile(sin2, (reps, 1))


DEFAULT_CFG = dict(tm=1024, tn_mix=512, tn_ffn=512, chunk=256, tm_ln=512, tk_down=1408, tn_proj=512,
                   tq_p=128, tk_p=512, tk_s=1024)


def _forward(x_prompt, x_sample, state_conv_mix, cache_k, cache_v, cache_idx_k, state_ffn_conv,
             mix_w_in, mix_conv_w, mix_w_out, attn_w_in, idx_k_norm_g, idx_k_norm_b, attn_w_out,
             ffn_w_in, ffn_conv_w, ffn_conv_b, ffn_w_down, ln1_g, ln1_b, ln2_g, ln2_b, cfg):
    bp, tp, d = x_prompt.shape
    bs, ts, _ = x_sample.shape
    past = cache_k.shape[2]
    depth = ln1_g.shape[0]
    d_ff = ffn_conv_b.shape[1]
    alpha = (2 * depth) ** 0.25
    topk_p = min(INDEX_TOPK, tp // 4)
    topk_s = min(INDEX_TOPK, (past + ts) // 4)
    tm = cfg["tm"]
    mix_w_out_b, attn_w_out_b, ffn_w_down_b = mix_w_out.astype(BF16), attn_w_out.astype(BF16), ffn_w_down.astype(BF16)

    streams = {
        "p": [x_prompt.reshape(bp * tp, d), x_prompt.reshape(bp * tp, d).astype(BF16), tp, bp],
        "s": [x_sample.reshape(bs * ts, d), x_sample.reshape(bs * ts, d).astype(BF16), ts, bs],
    }
    out = {k: {"conv": [], "k": [], "v": [], "ik": [], "ffn": []} for k in streams}

    def tables(name):
        seq_len, n_rows = streams[name][2], streams[name][0].shape[0]
        tile = _row_tile(n_rows, tm)
        if name == "p":
            pos = jnp.arange(seq_len, dtype=jnp.int32)
        else:
            pos = past + jnp.arange(seq_len, dtype=jnp.int32)
        return _rope_tables(pos, max(1, tile // seq_len))

    for i in range(depth):
        j = i // 2
        if i % 2 == 0:
            for name, (xf, xb, seq_len, n_seq) in streams.items():
                st = jnp.zeros((n_seq, 2, d), F32) if name == "p" else state_conv_mix[j]
                z, nst = _gated_conv_call(_mix_in_kernel, "mix_in_" + name, xb, mix_w_in, j, 3, mix_conv_w[j],
                                          None, st, seq_len, tm, cfg["tn_mix"], cfg["chunk"])
                out[name]["conv"].append(nst)
                xf, xb = _mm_ln_call(z, mix_w_out_b, j, xf, ln1_g[i], ln1_b[i], alpha, cfg["tm_ln"], d)
                streams[name][0], streams[name][1] = xf, xb
        else:
            w_in = attn_w_in[j].astype(BF16)
            splits = (Q_WIDTH, Q_WIDTH + KV_WIDTH, Q_WIDTH + 2 * KV_WIDTH,
                      Q_WIDTH + 2 * KV_WIDTH + IQ_WIDTH, Q_WIDTH + 2 * KV_WIDTH + IQ_WIDTH + IDX_DIM)
            w_idx = jnp.concatenate(
                [w_in[:, splits[3]:], jnp.zeros((d, IDX_DIM - N_IDX_HEADS), BF16)], axis=1)
            tn = cfg["tn_proj"]
            q_blocks = list(range(0, Q_WIDTH // tn)) + list(range(splits[2] // tn, splits[3] // tn))
            k_blocks = list(range(splits[0] // tn, splits[1] // tn))
            v_blocks = list(range(splits[1] // tn, splits[2] // tn))
            for name, (xf, xb, seq_len, n_seq) in streams.items():
                cos, sin = tables(name)
                qiq, = _proj_call(xb, w_in, q_blocks, cos, sin, True, [BF16], tm, tn, n_scaled=Q_WIDTH // tn)
                k_f, k_b = _proj_call(xb, w_in, k_blocks, cos, sin, True, [F32, BF16], tm, tn)
                v_f, v_b = _proj_call(xb, w_in, v_blocks, cos, sin, False, [F32, BF16], tm, tn)
                ik_f, ik_b, iw = _idx_proj_call(xb, w_idx, cos, sin, idx_k_norm_g[j], idx_k_norm_b[j], tm)
                out[name]["k"].append(k_f.reshape(n_seq, seq_len, N_KV_HEADS, HEAD_DIM))
                out[name]["v"].append(v_f.reshape(n_seq, seq_len, N_KV_HEADS, HEAD_DIM))
                out[name]["ik"].append(ik_f.reshape(n_seq, seq_len, IDX_DIM))
                w_rows = iw[:, :N_IDX_HEADS].reshape(n_seq, seq_len, N_IDX_HEADS).transpose(0, 2, 1)
                if name == "p":
                    o = _dsa_prompt_call(qiq, w_rows, ik_b.reshape(n_seq, seq_len, IDX_DIM),
                                         k_b.reshape(n_seq, seq_len, KV_WIDTH), v_b.reshape(n_seq, seq_len, KV_WIDTH),
                                         topk_p, cfg["tq_p"], min(cfg["tk_p"], seq_len))
                else:
                    w_rows = w_rows.reshape(n_seq, N_IDX_HEADS * seq_len // LANES, LANES)
                    o = _dsa_sample_call(qiq, w_rows, cache_idx_k[j],
                                         cache_k[j].reshape(n_seq, past * N_KV_HEADS, HEAD_DIM),
                                         cache_v[j].reshape(n_seq, past * N_KV_HEADS, HEAD_DIM),
                                         ik_b, k_b, v_b, topk_s, min(cfg["tk_s"], past))
                xf, xb = _mm_ln_call(o, attn_w_out_b, j, xf, ln1_g[i], ln1_b[i], alpha, cfg["tm_ln"], Q_WIDTH)
                streams[name][0], streams[name][1] = xf, xb

        for name, (xf, xb, seq_len, n_seq) in streams.items():
            st = jnp.zeros((n_seq, 2, d_ff), F32) if name == "p" else state_ffn_conv[i]
            h, nst = _gated_conv_call(_ffn_in_kernel, "ffn_in_" + name, xb, ffn_w_in, i, 2, ffn_conv_w[i],
                                      ffn_conv_b[i], st, seq_len, tm, cfg["tn_ffn"], cfg["chunk"])
            out[name]["ffn"].append(nst)
            xf, xb = _mm_ln_call(h, ffn_w_down_b, i, xf, ln2_g[i], ln2_b[i], alpha, cfg["tm_ln"], cfg["tk_down"])
            streams[name][0], streams[name][1] = xf, xb

    p, s = out["p"], out["s"]
    return (streams["p"][0].reshape(bp, tp, d), streams["s"][0].reshape(bs, ts, d),
            jnp.stack(p["conv"]), jnp.stack(p["k"]), jnp.stack(p["v"]), jnp.stack(p["ik"]), jnp.stack(p["ffn"]),
            jnp.stack(s["conv"]), jnp.stack(s["k"]), jnp.stack(s["v"]), jnp.stack(s["ik"]), jnp.stack(s["ffn"]))


def kernel(x_prompt, x_sample, state_conv_mix, cache_k, cache_v, cache_idx_k, state_ffn_conv, mix_w_in, mix_conv_w, mix_w_out, attn_w_in, idx_k_norm_g, idx_k_norm_b, attn_w_out, ffn_w_in, ffn_conv_w, ffn_conv_b, ffn_w_down, ln1_g, ln1_b, ln2_g, ln2_b):
    return _forward(x_prompt, x_sample, state_conv_mix, cache_k, cache_v, cache_idx_k, state_ffn_conv,
                    mix_w_in, mix_conv_w, mix_w_out, attn_w_in, idx_k_norm_g, idx_k_norm_b, attn_w_out,
                    ffn_w_in, ffn_conv_w, ffn_conv_b, ffn_w_down, ln1_g, ln1_b, ln2_g, ln2_b, DEFAULT_CFG)
```

```python
import functools

import jax
import jax.numpy as jnp
from jax import lax
from jax.experimental import pallas as pl
from jax.experimental.pallas import tpu as pltpu

CHUNK = 64
CHUNK_SHIFT = CHUNK.bit_length() - 1
assert 1 << CHUNK_SHIFT == CHUNK
N_HEADS = 16
N_KV_HEADS = 4
HEAD_DIM = 128
N_IDX_HEADS = 16
IDX_DIM = 128
INDEX_TOPK = 256
ROPE_THETA = 10000.0
LN_EPS = 1e-5
ATTN_SCALE = HEAD_DIM ** -0.5
IDX_W_SCALE = (N_IDX_HEADS ** -0.5) * (IDX_DIM ** -0.5)
Q_WIDTH = N_HEADS * HEAD_DIM
KV_WIDTH = N_KV_HEADS * HEAD_DIM
IQ_WIDTH = N_IDX_HEADS * IDX_DIM
GROUP = N_HEADS // N_KV_HEADS

LANES = 128
VMEM_LIMIT_BYTES = 56 * 1024 * 1024
MASK_BIAS = -1e30
MAX_INIT = -5e29
INT_MIN = -2 ** 31
LOG2_E = 1.4426950408889634

F32 = jnp.float32
BF16 = jnp.bfloat16


def _dot(a, b):
    return jnp.dot(a, b, preferred_element_type=F32)


def _dot_nt(a, b):
    return lax.dot_general(a, b, (((1,), (1,)), ((), ())), preferred_element_type=F32)


def _params(*sem):
    return pltpu.CompilerParams(dimension_semantics=sem, vmem_limit_bytes=VMEM_LIMIT_BYTES)


def _row_tile(n_rows, want):
    return want if n_rows % want == 0 else n_rows


def _conv3_rows(u, prev, w):
    rows = lax.broadcasted_iota(jnp.int32, u.shape, 0)
    p2, p1 = prev[0:1], prev[1:2]
    u1 = jnp.where(rows == 0, p1, pltpu.roll(u, 1, 0))
    u2 = jnp.where(rows == 0, p2, jnp.where(rows == 1, p1, pltpu.roll(u, 2, 0)))
    return w[0:1] * u2 + w[1:2] * u1 + w[2:3] * u


def _segmented_conv(u, cs, w_ref, st_ref, nst_ref, carry_ref, emit, *, seg, nseg, tiles_per_seq):
    w = w_ref[:, cs]
    if nseg == 1:
        emit(slice(0, seg), _conv3_rows(u, carry_ref[:, cs], w))
        tail = u[seg - 2:seg]
        carry_ref[:, cs] = tail
        nst_ref[0, :, cs] = tail
    else:
        for s in range(nseg):
            rs = slice(s * seg, (s + 1) * seg)
            us = u[rs]
            emit(rs, _conv3_rows(us, st_ref[s, :, cs], w))
            nst_ref[s, :, cs] = us[seg - 2:seg]


def _chunked_gated_conv(matmuls, epilogue, st_ref, carry_ref, n_cols, chunk, seg_kw):
    if seg_kw["nseg"] == 1:
        @pl.when(pl.program_id(1) % seg_kw["tiles_per_seq"] == 0)
        def _():
            carry_ref[...] = st_ref[0]

    chunks = [slice(c, c + chunk) for c in range(0, n_cols, chunk)]
    nxt = matmuls(chunks[0])
    for c, cs in enumerate(chunks):
        cur = nxt
        if c + 1 < len(chunks):
            nxt = matmuls(chunks[c + 1])
        epilogue(cs, cur)


def _mix_in_kernel(x_ref, wb_ref, wc_ref, wv_ref, cw_ref, st_ref, z_ref, nst_ref, carry_ref, *, chunk, **seg_kw):
    x = x_ref[...]

    def matmuls(cs):
        return _dot(x, wc_ref[:, cs]) * _dot(x, wv_ref[:, cs]), _dot(x, wb_ref[:, cs])

    def epilogue(cs, res):
        u, b = res

        def emit(rs, conv):
            z_ref[rs, cs] = (b[rs] * conv).astype(z_ref.dtype)

        _segmented_conv(u, cs, cw_ref, st_ref, nst_ref, carry_ref, emit, **seg_kw)

    _chunked_gated_conv(matmuls, epilogue, st_ref, carry_ref, z_ref.shape[1], chunk, seg_kw)


def _ffn_in_kernel(x_ref, wg_ref, wu_ref, cw_ref, cb_ref, st_ref, h_ref, nst_ref, carry_ref, *, chunk, **seg_kw):
    x = x_ref[...]

    def matmuls(cs):
        return _dot(x, wg_ref[:, cs]), _dot(x, wu_ref[:, cs])

    def epilogue(cs, res):
        g, up = res
        cb = cb_ref[:, cs]

        def emit(rs, conv):
            a = conv + cb
            h_ref[rs, cs] = (a * jax.nn.sigmoid(a) * up[rs]).astype(h_ref.dtype)

        _segmented_conv(g, cs, cw_ref, st_ref, nst_ref, carry_ref, emit, **seg_kw)

    _chunked_gated_conv(matmuls, epilogue, st_ref, carry_ref, h_ref.shape[1], chunk, seg_kw)


def _gated_conv_call(body, name, x, w, layer, n_branches, conv_w, conv_b, state, seq_len, tm, tn, chunk):
    n, k = x.shape
    c = w.shape[2] // n_branches
    tm = _row_tile(n, tm)
    tn = tn if c % tn == 0 else c
    nct, nrt = c // tn, n // tm
    if seq_len >= tm:
        assert seq_len % tm == 0
        seg, nseg, tiles_per_seq = tm, 1, seq_len // tm
        st_spec = pl.BlockSpec((1, 2, tn), lambda j, i: (i // tiles_per_seq, 0, j))
    else:
        assert tm % seq_len == 0
        seg, nseg, tiles_per_seq = seq_len, tm // seq_len, 1
        st_spec = pl.BlockSpec((nseg, 2, tn), lambda j, i: (i, 0, j))
    chunk = chunk if (tn % chunk == 0 and nseg == 1) else tn
    w_specs = [pl.BlockSpec((None, k, tn), functools.partial(lambda j, i, o: (layer, 0, j + o), o=b * nct))
               for b in range(n_branches)]
    vec_specs = [pl.BlockSpec((conv_w.shape[0], tn), lambda j, i: (0, j))]
    vec_args = [conv_w]
    if conv_b is not None:
        vec_specs.append(pl.BlockSpec((1, tn), lambda j, i: (0, j)))
        vec_args.append(conv_b.reshape(1, c))
    def with_cast_weights(*refs):
        w_f32, rest, w_bf16 = refs[1:1 + n_branches], refs[1 + n_branches:-n_branches], refs[-n_branches:]

        @pl.when(pl.program_id(1) == 0)
        def _():
            for src, dst in zip(w_f32, w_bf16):
                dst[...] = src[...].astype(dst.dtype)

        body(refs[0], *w_bf16, *rest, chunk=chunk, seg=seg, nseg=nseg, tiles_per_seq=tiles_per_seq)

    return pl.pallas_call(
        with_cast_weights,
        grid=(nct, nrt),
        in_specs=[pl.BlockSpec((tm, k), lambda j, i: (i, 0))] + w_specs + vec_specs + [st_spec],
        out_specs=[pl.BlockSpec((tm, tn), lambda j, i: (i, j)), st_spec],
        out_shape=[jax.ShapeDtypeStruct((n, c), BF16), jax.ShapeDtypeStruct(state.shape, F32)],
        scratch_shapes=[pltpu.VMEM((2, tn), F32)] + [pltpu.VMEM((k, tn), BF16)] * n_branches,
        compiler_params=_params("arbitrary", "arbitrary"),
        name=name,
    )(x, *([w] * n_branches), *vec_args, state)


def _mm_ln_kernel(a_ref, w_ref, x_ref, g_ref, b_ref, o_ref, ob_ref, *acc, alpha, nk):
    def finish(y):
        r = alpha * x_ref[...] + y
        mu = jnp.mean(r, axis=-1, keepdims=True)
        rc = r - mu
        var = jnp.mean(rc * rc, axis=-1, keepdims=True)
        out = rc * lax.rsqrt(var + LN_EPS) * g_ref[...] + b_ref[...]
        o_ref[...] = out
        ob_ref[...] = out.astype(ob_ref.dtype)

    part = _dot(a_ref[...], w_ref[...])
    if nk == 1:
        finish(part)
    else:
        acc_ref, = acc
        kk = pl.program_id(1)

        @pl.when(kk == 0)
        def _():
            acc_ref[...] = part

        @pl.when(jnp.logical_and(kk > 0, kk < nk - 1))
        def _():
            acc_ref[...] += part

        @pl.when(kk == nk - 1)
        def _():
            finish(acc_ref[...] + part)


def _mm_ln_call(a, w, layer, x, g, b, alpha, tm, tk):
    n, k = a.shape
    d = w.shape[2]
    tm = _row_tile(n, tm)
    tk = tk if k % tk == 0 else k
    nk = k // tk
    row_mode = dict(pipeline_mode=pl.Buffered(1)) if nk > 1 else {}
    return pl.pallas_call(
        functools.partial(_mm_ln_kernel, alpha=alpha, nk=nk),
        grid=(n // tm, nk),
        in_specs=[pl.BlockSpec((tm, tk), lambda i, kk: (i, kk)),
                  pl.BlockSpec((None, tk, d), lambda i, kk: (layer, kk, 0)),
                  pl.BlockSpec((tm, d), lambda i, kk: (i, 0), **row_mode),
                  pl.BlockSpec((1, d), lambda i, kk: (0, 0)),
                  pl.BlockSpec((1, d), lambda i, kk: (0, 0))],
        out_specs=[pl.BlockSpec((tm, d), lambda i, kk: (i, 0), **row_mode),
                   pl.BlockSpec((tm, d), lambda i, kk: (i, 0), **row_mode)],
        out_shape=[jax.ShapeDtypeStruct((n, d), F32), jax.ShapeDtypeStruct((n, d), BF16)],
        scratch_shapes=[pltpu.VMEM((tm, d), F32)] if nk > 1 else [],
        compiler_params=_params("arbitrary", "arbitrary"),
        name=f"mm_ln_k{k}_rows{n}",
    )(a, w, x, g.reshape(1, d), b.reshape(1, d))


def _rope_head(y, cos, sin):
    return y * cos + pltpu.roll(y, HEAD_DIM // 2, 1) * sin


def _proj_kernel(x_ref, w_ref, cos_ref, sin_ref, *outs, rope, n_scaled):
    y = _dot(x_ref[...], w_ref[...])
    cos, sin = cos_ref[...], sin_ref[...]
    if n_scaled:
        scale = jnp.where(pl.program_id(0) < n_scaled, ATTN_SCALE * LOG2_E, 1.0)
    for h in range(y.shape[1] // HEAD_DIM):
        hs = slice(h * HEAD_DIM, (h + 1) * HEAD_DIM)
        yh = y[:, hs]
        if rope:
            yh = _rope_head(yh, cos, sin)
        if n_scaled:
            yh = yh * scale
        for o_ref in outs:
            o_ref[:, hs] = yh.astype(o_ref.dtype)


def _proj_call(x, w, col_blocks, cos, sin, rope, out_dtypes, tm, tn, n_scaled=0):
    n, k = x.shape
    tm = _row_tile(n, tm)
    nrt = n // tm
    tab_tiles = cos.shape[0] // tm
    blocks = jnp.asarray(col_blocks, jnp.int32)
    width = len(col_blocks) * tn
    grid_spec = pltpu.PrefetchScalarGridSpec(
        num_scalar_prefetch=1,
        grid=(len(col_blocks), nrt),
        in_specs=[pl.BlockSpec((tm, k), lambda j, i, blk: (i, 0)),
                  pl.BlockSpec((k, tn), lambda j, i, blk: (0, blk[j])),
                  pl.BlockSpec((tm, HEAD_DIM), lambda j, i, blk: (i % tab_tiles, 0)),
                  pl.BlockSpec((tm, HEAD_DIM), lambda j, i, blk: (i % tab_tiles, 0))],
        out_specs=[pl.BlockSpec((tm, tn), lambda j, i, blk: (i, j)) for _ in out_dtypes],
    )

    def body(blk_ref, *refs):
        del blk_ref
        _proj_kernel(*refs, rope=rope, n_scaled=n_scaled)

    return pl.pallas_call(
        body,
        grid_spec=grid_spec,
        out_shape=[jax.ShapeDtypeStruct((n, width), dt) for dt in out_dtypes],
        compiler_params=_params("arbitrary", "arbitrary"),
        name=f"proj_{len(col_blocks)}blk_rope{int(rope)}_rows{n}",
    )(blocks, x, w, cos, sin)


def _idx_proj_kernel(x_ref, w_ref, cos_ref, sin_ref, g_ref, b_ref, ik_ref, ikb_ref, iw_ref):
    y = _dot(x_ref[...], w_ref[...])
    a = y[:, :IDX_DIM]
    mu = jnp.mean(a, axis=-1, keepdims=True)
    ac = a - mu
    var = jnp.mean(ac * ac, axis=-1, keepdims=True)
    a = ac * lax.rsqrt(var + LN_EPS) * g_ref[...] + b_ref[...]
    a = _rope_head(a, cos_ref[...], sin_ref[...])
    ik_ref[...] = a
    ikb_ref[...] = a.astype(ikb_ref.dtype)
    iw_ref[...] = y[:, IDX_DIM:] * IDX_W_SCALE


def _idx_proj_call(x, w, cos, sin, g, b, tm):
    n, k = x.shape
    tm = _row_tile(n, tm)
    tab_tiles = cos.shape[0] // tm
    row = lambda i: (i, 0)
    tab = lambda i: (i % tab_tiles, 0)
    fixed = lambda i: (0, 0)
    return pl.pallas_call(
        _idx_proj_kernel,
        grid=(n // tm,),
        in_specs=[pl.BlockSpec((tm, k), row), pl.BlockSpec((k, 2 * IDX_DIM), fixed),
                  pl.BlockSpec((tm, IDX_DIM), tab), pl.BlockSpec((tm, IDX_DIM), tab),
                  pl.BlockSpec((1, IDX_DIM), fixed), pl.BlockSpec((1, IDX_DIM), fixed)],
        out_specs=[pl.BlockSpec((tm, IDX_DIM), row)] * 3,
        out_shape=[jax.ShapeDtypeStruct((n, IDX_DIM), F32), jax.ShapeDtypeStruct((n, IDX_DIM), BF16),
                   jax.ShapeDtypeStruct((n, IDX_DIM), F32)],
        compiler_params=_params("arbitrary"),
        name=f"idx_proj_rows{n}",
    )(x, w, cos, sin, g.reshape(1, IDX_DIM), b.reshape(1, IDX_DIM))


def _stack_heads(x, first, count, width):
    return jnp.concatenate([x[:, (first + r) * width:(first + r + 1) * width] for r in range(count)], axis=0)


def _admissible(kpos_first, n_keys, qpos_first, tq):
    rows = lax.broadcasted_iota(jnp.int32, (n_keys, LANES), 0)
    lanes = lax.broadcasted_iota(jnp.int32, (n_keys, LANES), 1)
    return ((kpos_first + rows) >> CHUNK_SHIFT) <= ((qpos_first + (lanes & (tq - 1))) >> CHUNK_SHIFT)


def _index_keys(ik_t, iq_rows, w_rows, adm, tq):
    s = _dot_nt(ik_t, iq_rows)
    acc = jnp.maximum(s[:, :LANES], 0.0) * w_rows[0:1]
    for j in range(1, s.shape[1] // LANES):
        acc = acc + jnp.maximum(s[:, j * LANES:(j + 1) * LANES], 0.0) * w_rows[j:j + 1]
    if tq < LANES:
        acc = acc + pltpu.roll(acc, tq, 1)
    acc = jnp.where(adm, acc, -jnp.inf)
    bits = pltpu.bitcast(acc, jnp.int32)
    return bits ^ ((bits >> 31) & 0x7FFFFFFF)


COUNT_ROWS = 64
MAX_EXACT_BF16_COUNT = 256
I16_MIN = -2 ** 15


def _fold_rows(x, tq):
    half = x.shape[0] // 2
    lanes = lax.broadcasted_iota(jnp.int32, (half, LANES), 1)
    return jnp.where(lanes < tq, x[:half], x[half:])


def _split_key(key):
    return (key >> 16).astype(jnp.int16), ((key & 0xFFFF) + I16_MIN).astype(jnp.int16)


def _count16(mask):
    ones = jnp.where(mask, jnp.bfloat16(1), jnp.bfloat16(0)).reshape(-1, COUNT_ROWS, LANES)
    c = ones[0]
    for i in range(1, ones.shape[0]):
        c = c + ones[i]
    return c


def _kth_largest_key(hi_ref, lo_ref, nkt, topk, tq):
    assert hi_ref.shape[1] // COUNT_ROWS * hi_ref.shape[0] <= MAX_EXACT_BF16_COUNT

    def total(count_tile):
        c = lax.fori_loop(0, nkt, lambda kt, c: c + count_tile(kt), jnp.zeros((COUNT_ROWS, LANES), BF16))
        cnt = jnp.sum(c.astype(F32), axis=0, keepdims=True)
        if tq < LANES:
            cnt = cnt + pltpu.roll(cnt, tq, 1)
        return cnt

    def search(ref, base):
        def bit_body(bi, t):
            cand = t | jnp.left_shift(jnp.int32(1), 15 - bi)
            c16 = (cand + I16_MIN).astype(jnp.int16)
            cnt = base + total(lambda kt: _count16(ref[kt] >= c16))
            return jnp.where(cnt >= topk, cand, t)

        return lax.fori_loop(0, 16, bit_body, jnp.zeros((1, LANES), jnp.int32))

    t_hi = search(hi_ref, 0.0) + I16_MIN
    t_hi16 = t_hi.astype(jnp.int16)
    above = total(lambda kt: _count16(hi_ref[kt] > t_hi16))

    def mask_lo(kt, carry):
        lo_ref[kt] = jnp.where(hi_ref[kt] == t_hi16, lo_ref[kt], jnp.int16(I16_MIN))
        return carry

    lax.fori_loop(0, nkt, mask_lo, 0)
    t_lo = search(lo_ref, above)
    return jnp.left_shift(t_hi, 16) | t_lo


def _select_bias(keys, thr, adm):
    return jnp.where(jnp.logical_and(keys >= thr, adm), 0.0, MASK_BIAS)


def _group_q_rows(q_ref, g):
    return jnp.concatenate([q_ref[:, (g * GROUP + r) * HEAD_DIM:(g * GROUP + r + 1) * HEAD_DIM]
                            for r in range(GROUP)], axis=0)


def _attend_groups(ms, ls, bias, k_of, v_of, q_ref, update_acc):
    tq = q_ref.shape[0]
    bias_rep = jnp.concatenate([bias] * (GROUP * tq // LANES), axis=1)

    def scores(g):
        return _dot_nt(k_of(g), _group_q_rows(q_ref, g)) + bias_rep

    new_m, new_l = [], []
    s_next = scores(0)
    for g in range(N_KV_HEADS):
        s = s_next
        if g + 1 < N_KV_HEADS:
            s_next = scores(g + 1)
        m, l, alpha, pv = _attend_tile(ms[g], ls[g], s, v_of(g))
        update_acc(g, alpha, pv)
        new_m.append(m)
        new_l.append(l)
    return tuple(new_m), tuple(new_l)


def _attend_tile(m, l, s, v_t):
    m_new = jnp.maximum(m, jnp.max(s, axis=0, keepdims=True))
    p = jnp.exp2(s - m_new)
    alpha = jnp.exp2(m - m_new)
    l_new = alpha * l + jnp.sum(p, axis=0, keepdims=True)
    pv = lax.dot_general(v_t, p.astype(BF16), (((0,), (0,)), ((), ())), preferred_element_type=F32)
    return m_new, l_new, alpha, pv


def _attend_init(tq):
    n = GROUP * tq
    return (tuple(jnp.full((1, n), MAX_INIT, F32) for _ in range(N_KV_HEADS)),
            tuple(jnp.zeros((1, n), F32) for _ in range(N_KV_HEADS)))


def _write_group(o_ref, g, l, acc, tq):
    o = (acc / l).T
    for r in range(GROUP):
        hs = slice((g * GROUP + r) * HEAD_DIM, (g * GROUP + r + 1) * HEAD_DIM)
        o_ref[:, hs] = o[r * tq:(r + 1) * tq].astype(o_ref.dtype)


def _dsa_prompt_kernel(q_ref, iq_ref, w_ref, ik_ref, k_ref, v_ref, o_ref,
                       key_ref, hi_ref, lo_ref, bias_ref, acc_ref, *, tq, tk, lk, topk):
    qpos_first = pl.program_id(1) * tq
    kend = jnp.minimum(((qpos_first + tq - 1) // CHUNK + 1) * CHUNK, lk)
    nkt = (kend + tk - 1) // tk

    def adm(kt):
        return _admissible(kt * tk, tk, qpos_first, tq)

    def key_rows(kt):
        return pl.ds(pl.multiple_of(kt * tk, tk), tk)

    iq_rows = _stack_heads(iq_ref[...], 0, N_IDX_HEADS, IDX_DIM)
    w_rows = w_ref[0]

    def idx_body(kt, carry):
        keys = _index_keys(ik_ref[0, key_rows(kt), :], iq_rows, w_rows, adm(kt), tq)
        key_ref[kt] = keys
        hi_ref[kt], lo_ref[kt] = _split_key(keys)
        return carry

    lax.fori_loop(0, nkt, idx_body, 0)
    thr = _kth_largest_key(hi_ref, lo_ref, nkt, topk, tq)

    def bias_body(kt, carry):
        bias_ref[kt] = _select_bias(key_ref[kt], thr, adm(kt))
        return carry

    lax.fori_loop(0, nkt, bias_body, 0)

    acc_ref[...] = jnp.zeros(acc_ref.shape, F32)

    def update_acc(g, alpha, pv):
        acc_ref[g] = alpha * acc_ref[g] + pv

    def att_body(kt, carry):
        ks = key_rows(kt)
        return _attend_groups(*carry, bias_ref[kt],
                              lambda g: k_ref[0, ks, g * HEAD_DIM:(g + 1) * HEAD_DIM],
                              lambda g: v_ref[0, ks, g * HEAD_DIM:(g + 1) * HEAD_DIM], q_ref, update_acc)

    _, ls = lax.fori_loop(0, nkt, att_body, _attend_init(tq))
    for g in range(N_KV_HEADS):
        _write_group(o_ref, g, ls[g], acc_ref[g], tq)


def _dsa_prompt_call(qiq, w_rows, ik, k, v, topk, tq, tk):
    n_seq, lk = ik.shape[:2]
    assert lk % tk == 0 and lk % tq == 0 and tq == LANES
    nqb = lk // tq
    qrow = lambda b, i: (b * nqb + i, 0)
    seq = lambda b, i: (b, 0, 0)
    return pl.pallas_call(
        functools.partial(_dsa_prompt_kernel, tq=tq, tk=tk, lk=lk, topk=topk),
        grid=(n_seq, nqb),
        in_specs=[pl.BlockSpec((tq, Q_WIDTH), qrow),
                  pl.BlockSpec((tq, IQ_WIDTH), lambda b, i: (b * nqb + i, Q_WIDTH // IQ_WIDTH)),
                  pl.BlockSpec((1, N_IDX_HEADS, tq), lambda b, i: (b, 0, i)),
                  pl.BlockSpec((1, lk, IDX_DIM), seq),
                  pl.BlockSpec((1, lk, KV_WIDTH), seq),
                  pl.BlockSpec((1, lk, KV_WIDTH), seq)],
        out_specs=pl.BlockSpec((tq, Q_WIDTH), qrow),
        out_shape=jax.ShapeDtypeStruct((n_seq * lk, Q_WIDTH), BF16),
        scratch_shapes=[pltpu.VMEM((lk // tk, tk, LANES), jnp.int32),
                        pltpu.VMEM((lk // tk, tk, LANES), jnp.int16), pltpu.VMEM((lk // tk, tk, LANES), jnp.int16),
                        pltpu.VMEM((lk // tk, tk, LANES), F32),
                        pltpu.VMEM((N_KV_HEADS, HEAD_DIM, GROUP * tq), F32)],
        compiler_params=_params("arbitrary", "arbitrary"),
        name="dsa_prompt",
    )(qiq, qiq, w_rows, ik, k, v)


def _dsa_sample_kernel(q_ref, iq_ref, w_ref, cik_ref, ck_ref, cv_ref, nik_ref, nk_ref, nv_ref, o_ref,
                       key_ref, keyn_ref, hi_ref, lo_ref, *, tq, tk, past, topk):
    nct = past // tk
    iq_rows = _stack_heads(iq_ref[...], 0, N_IDX_HEADS, IDX_DIM)
    w_rows = w_ref[0]

    def adm_cache(kt):
        return _admissible(kt * tk, tk, past, tq)

    adm_new = _admissible(past, tq, past, tq)

    for kt in range(nct):
        ik_t = cik_ref[0, kt * tk:(kt + 1) * tk, :].astype(BF16)
        keys = _index_keys(ik_t, iq_rows, w_rows, adm_cache(kt), tq)
        key_ref[kt] = keys
        hi_ref[kt], lo_ref[kt] = _split_key(_fold_rows(keys, tq))
    keys = _index_keys(nik_ref[...], iq_rows, w_rows, adm_new, tq)
    keyn_ref[...] = keys
    pad = jnp.full(((tk - tq) // 2, LANES), INT_MIN, jnp.int32)
    hi_ref[nct], lo_ref[nct] = _split_key(jnp.concatenate([_fold_rows(keys, tq), pad], axis=0))
    thr = _kth_largest_key(hi_ref, lo_ref, nct + 1, topk, tq)

    carry = _attend_init(tq)
    accs = [jnp.zeros((HEAD_DIM, GROUP * tq), F32) for _ in range(N_KV_HEADS)]

    def update_acc(g, alpha, pv):
        accs[g] = alpha * accs[g] + pv

    for kt in range(nct):
        def cache_rows(ref, g, kt=kt):
            return ref[0, pl.ds(kt * tk * N_KV_HEADS + g, tk, stride=N_KV_HEADS), :].astype(BF16)

        carry = _attend_groups(*carry, _select_bias(key_ref[kt], thr, adm_cache(kt)),
                               functools.partial(cache_rows, ck_ref), functools.partial(cache_rows, cv_ref),
                               q_ref, update_acc)
    carry = _attend_groups(*carry, _select_bias(keyn_ref[...], thr, adm_new),
                           lambda g: nk_ref[:, g * HEAD_DIM:(g + 1) * HEAD_DIM],
                           lambda g: nv_ref[:, g * HEAD_DIM:(g + 1) * HEAD_DIM], q_ref, update_acc)
    for g in range(N_KV_HEADS):
        _write_group(o_ref, g, carry[1][g], accs[g], tq)


def _dsa_sample_call(qiq, w_rows, cache_ik, cache_k, cache_v, new_ik, new_k, new_v, topk, tk):
    n_seq, past = cache_ik.shape[:2]
    tq = new_ik.shape[0] // n_seq
    assert past % tk == 0 and 2 * tq == LANES and (tk // 2) % COUNT_ROWS == 0
    qrow = lambda b: (b, 0)
    seq = lambda b: (b, 0, 0)
    return pl.pallas_call(
        functools.partial(_dsa_sample_kernel, tq=tq, tk=tk, past=past, topk=topk),
        grid=(n_seq,),
        in_specs=[pl.BlockSpec((tq, Q_WIDTH), qrow),
                  pl.BlockSpec((tq, IQ_WIDTH), lambda b: (b, Q_WIDTH // IQ_WIDTH)),
                  pl.BlockSpec((1,) + w_rows.shape[1:], seq),
                  pl.BlockSpec((1, past, IDX_DIM), seq),
                  pl.BlockSpec((1, past * N_KV_HEADS, HEAD_DIM), seq),
                  pl.BlockSpec((1, past * N_KV_HEADS, HEAD_DIM), seq),
                  pl.BlockSpec((tq, IDX_DIM), qrow),
                  pl.BlockSpec((tq, KV_WIDTH), qrow),
                  pl.BlockSpec((tq, KV_WIDTH), qrow)],
        out_specs=pl.BlockSpec((tq, Q_WIDTH), qrow),
        out_shape=jax.ShapeDtypeStruct((n_seq * tq, Q_WIDTH), BF16),
        scratch_shapes=[pltpu.VMEM((past // tk, tk, LANES), jnp.int32), pltpu.VMEM((tq, LANES), jnp.int32),
                        pltpu.VMEM((past // tk + 1, tk // 2, LANES), jnp.int16),
                        pltpu.VMEM((past // tk + 1, tk // 2, LANES), jnp.int16)],
        compiler_params=_params("arbitrary"),
        name="dsa_sample",
    )(qiq, qiq, w_rows, cache_ik, cache_k, cache_v, new_ik, new_k, new_v)


def _rope_tables(pos, reps):
    half = HEAD_DIM // 2
    inv_freq = ROPE_THETA ** (-jnp.arange(half, dtype=F32) / half)
    ang = pos.astype(F32)[:, None] * inv_freq[None, :]
    cos, sin = jnp.cos(ang), jnp.sin(ang)
    cos2 = jnp.concatenate([cos, cos], axis=1)
    sin2 = jnp.concatenate([-sin, sin], axis=1)
    return jnp.tile(cos2, (reps, 1)), jnp.tile(sin2, (reps, 1))


DEFAULT_CFG = dict(tm=1024, tn_mix=512, tn_ffn=512, chunk=256, tm_ln=512, tm_down=1024, tk_down=512, tn_proj=512,
                   tq_p=128, tk_p=512, tk_s=1024)


def _forward(x_prompt, x_sample, state_conv_mix, cache_k, cache_v, cache_idx_k, state_ffn_conv,
             mix_w_in, mix_conv_w, mix_w_out, attn_w_in, idx_k_norm_g, idx_k_norm_b, attn_w_out,
             ffn_w_in, ffn_conv_w, ffn_conv_b, ffn_w_down, ln1_g, ln1_b, ln2_g, ln2_b, cfg):
    bp, tp, d = x_prompt.shape
    bs, ts, _ = x_sample.shape
    past = cache_k.shape[2]
    depth = ln1_g.shape[0]
    d_ff = ffn_conv_b.shape[1]
    alpha = (2 * depth) ** 0.25
    topk_p = min(INDEX_TOPK, tp // 4)
    topk_s = min(INDEX_TOPK, (past + ts) // 4)
    tm = cfg["tm"]
    mix_w_out_b, attn_w_out_b, ffn_w_down_b = mix_w_out.astype(BF16), attn_w_out.astype(BF16), ffn_w_down.astype(BF16)

    streams = {
        "p": [x_prompt.reshape(bp * tp, d), x_prompt.reshape(bp * tp, d).astype(BF16), tp, bp],
        "s": [x_sample.reshape(bs * ts, d), x_sample.reshape(bs * ts, d).astype(BF16), ts, bs],
    }
    out = {k: {"conv": [], "k": [], "v": [], "ik": [], "ffn": []} for k in streams}

    def tables(name):
        seq_len, n_rows = streams[name][2], streams[name][0].shape[0]
        tile = _row_tile(n_rows, tm)
        if name == "p":
            pos = jnp.arange(seq_len, dtype=jnp.int32)
        else:
            pos = past + jnp.arange(seq_len, dtype=jnp.int32)
        return _rope_tables(pos, max(1, tile // seq_len))

    for i in range(depth):
        j = i // 2
        if i % 2 == 0:
            for name, (xf, xb, seq_len, n_seq) in streams.items():
                st = jnp.zeros((n_seq, 2, d), F32) if name == "p" else state_conv_mix[j]
                z, nst = _gated_conv_call(_mix_in_kernel, "mix_in_" + name, xb, mix_w_in, j, 3, mix_conv_w[j],
                                          None, st, seq_len, tm, cfg["tn_mix"], cfg["chunk"])
                out[name]["conv"].append(nst)
                xf, xb = _mm_ln_call(z, mix_w_out_b, j, xf, ln1_g[i], ln1_b[i], alpha, cfg["tm_ln"], d)
                streams[name][0], streams[name][1] = xf, xb
        else:
            w_in = attn_w_in[j].astype(BF16)
            splits = (Q_WIDTH, Q_WIDTH + KV_WIDTH, Q_WIDTH + 2 * KV_WIDTH,
                      Q_WIDTH + 2 * KV_WIDTH + IQ_WIDTH, Q_WIDTH + 2 * KV_WIDTH + IQ_WIDTH + IDX_DIM)
            w_idx = jnp.concatenate(
                [w_in[:, splits[3]:], jnp.zeros((d, IDX_DIM - N_IDX_HEADS), BF16)], axis=1)
            tn = cfg["tn_proj"]
            q_blocks = list(range(0, Q_WIDTH // tn)) + list(range(splits[2] // tn, splits[3] // tn))
            k_blocks = list(range(splits[0] // tn, splits[1] // tn))
            v_blocks = list(range(splits[1] // tn, splits[2] // tn))
            for name, (xf, xb, seq_len, n_seq) in streams.items():
                cos, sin = tables(name)
                qiq, = _proj_call(xb, w_in, q_blocks, cos, sin, True, [BF16], tm, tn, n_scaled=Q_WIDTH // tn)
                k_f, k_b = _proj_call(xb, w_in, k_blocks, cos, sin, True, [F32, BF16], tm, tn)
                v_f, v_b = _proj_call(xb, w_in, v_blocks, cos, sin, False, [F32, BF16], tm, tn)
                ik_f, ik_b, iw = _idx_proj_call(xb, w_idx, cos, sin, idx_k_norm_g[j], idx_k_norm_b[j], tm)
                out[name]["k"].append(k_f.reshape(n_seq, seq_len, N_KV_HEADS, HEAD_DIM))
                out[name]["v"].append(v_f.reshape(n_seq, seq_len, N_KV_HEADS, HEAD_DIM))
                out[name]["ik"].append(ik_f.reshape(n_seq, seq_len, IDX_DIM))
                w_rows = iw[:, :N_IDX_HEADS].reshape(n_seq, seq_len, N_IDX_HEADS).transpose(0, 2, 1)
                if name == "p":
                    o = _dsa_prompt_call(qiq, w_rows, ik_b.reshape(n_seq, seq_len, IDX_DIM),
                                         k_b.reshape(n_seq, seq_len, KV_WIDTH), v_b.reshape(n_seq, seq_len, KV_WIDTH),
                                         topk_p, cfg["tq_p"], min(cfg["tk_p"], seq_len))
                else:
                    w_rows = w_rows.reshape(n_seq, N_IDX_HEADS * seq_len // LANES, LANES)
                    o = _dsa_sample_call(qiq, w_rows, cache_idx_k[j],
                                         cache_k[j].reshape(n_seq, past * N_KV_HEADS, HEAD_DIM),
                                         cache_v[j].reshape(n_seq, past * N_KV_HEADS, HEAD_DIM),
                                         ik_b, k_b, v_b, topk_s, min(cfg["tk_s"], past))
                xf, xb = _mm_ln_call(o, attn_w_out_b, j, xf, ln1_g[i], ln1_b[i], alpha, cfg["tm_ln"], Q_WIDTH)
                streams[name][0], streams[name][1] = xf, xb

        for name, (xf, xb, seq_len, n_seq) in streams.items():
            st = jnp.zeros((n_seq, 2, d_ff), F32) if name == "p" else state_ffn_conv[i]
            h, nst = _gated_conv_call(_ffn_in_kernel, "ffn_in_" + name, xb, ffn_w_in, i, 2, ffn_conv_w[i],
                                      ffn_conv_b[i], st, seq_len, tm, cfg["tn_ffn"], cfg["chunk"])
            out[name]["ffn"].append(nst)
            xf, xb = _mm_ln_call(h, ffn_w_down_b, i, xf, ln2_g[i], ln2_b[i], alpha, cfg["tm_down"], cfg["tk_down"])
            streams[name][0], streams[name][1] = xf, xb

    p, s = out["p"], out["s"]
    return (streams["p"][0].reshape(bp, tp, d), streams["s"][0].reshape(bs, ts, d),
            jnp.stack(p["conv"]), jnp.stack(p["k"]), jnp.stack(p["v"]), jnp.stack(p["ik"]), jnp.stack(p["ffn"]),
            jnp.stack(s["conv"]), jnp.stack(s["k"]), jnp.stack(s["v"]), jnp.stack(s["ik"]), jnp.stack(s["ffn"]))


def kernel(x_prompt, x_sample, state_conv_mix, cache_k, cache_v, cache_idx_k, state_ffn_conv, mix_w_in, mix_conv_w, mix_w_out, attn_w_in, idx_k_norm_g, idx_k_norm_b, attn_w_out, ffn_w_in, ffn_conv_w, ffn_conv_b, ffn_w_down, ln1_g, ln1_b, ln2_g, ln2_b):
    return _forward(x_prompt, x_sample, state_conv_mix, cache_k, cache_v, cache_idx_k, state_ffn_conv,
                    mix_w_in, mix_conv_w, mix_w_out, attn_w_in, idx_k_norm_g, idx_k_norm_b, attn_w_out,
                    ffn_w_in, ffn_conv_w, ffn_conv_b, ffn_w_down, ln1_g, ln1_b, ln2_g, ln2_b, DEFAULT_CFG)
```
